```python
import jax, jax.numpy as jnp
from jax import lax
import numpy as np

D_MODEL = 1024
BATCH = 16
SEQ = 2048
DEPTH = 4

D_MIX = D_MODEL
GROUP = D_MIX // 4
MLSTM_HEADS = 4
MLSTM_DH = GROUP // MLSTM_HEADS
CHUNK = 64
POOL_WINDOWS = (2, 4, 8, 16)
POOL_GROUPS = len(POOL_WINDOWS)
POOL_GC = GROUP // POOL_GROUPS
CONV_K = 31
SCONV_K = 3
D_FF = 4 * D_MODEL
EPS = 1e-6
IN_SIZES = (GROUP, GROUP, GROUP, GROUP, MLSTM_HEADS, MLSTM_HEADS,
            GROUP,
            GROUP, GROUP,
            GROUP, GROUP, GROUP)
D_IN = sum(IN_SIZES)
IN_SPLITS = tuple(int(s) for s in np.cumsum(IN_SIZES)[:-1])

kernel_name = "hybrid_parallel_mlstm_pool_conformer_shortconv"


def rmsnorm(x, g):
    xf = x.astype(jnp.float32)
    y = xf * lax.rsqrt(jnp.mean(xf * xf, axis=-1, keepdims=True) + EPS)
    return (y * g.astype(jnp.float32)).astype(x.dtype)


def causal_dwconv(u, w):
    k, c = w.shape
    return lax.conv_general_dilated(
        u, w[:, None, :].astype(u.dtype), window_strides=(1,), padding=[(k - 1, 0)],
        dimension_numbers=("NWC", "WIO", "NWC"), feature_group_count=c)


def mlstm_mixer(q, k, v, o_pre, i_pre, f_pre, i_bias, f_bias, norm_gain):
    b, s, _ = q.shape
    nc = s // CHUNK
    dt = q.dtype

    def heads(t):
        return t.astype(jnp.float32).reshape(b, nc, CHUNK, MLSTM_HEADS, MLSTM_DH).transpose(1, 0, 3, 2, 4)

    def gates(t):
        return t.astype(jnp.float32).reshape(b, nc, CHUNK, MLSTM_HEADS).transpose(1, 0, 3, 2)

    qh = heads(q)
    kh = heads(k) * (MLSTM_DH ** -0.5)
    vh = heads(v)
    ig = gates(i_pre + i_bias)
    lf = jax.nn.log_sigmoid(gates(f_pre + f_bias))
    causal = jnp.tril(jnp.ones((CHUNK, CHUNK), dtype=bool))

    def step(carry, xs):
        c_prev, n_prev, m_prev = carry
        qc, kc, vc, igc, lfc = xs
        bcum = jnp.cumsum(lfc, axis=-1)
        dmat = bcum[..., :, None] - bcum[..., None, :] + igc[..., None, :]
        dmat = jnp.where(causal, dmat, -jnp.inf)
        m_inter = bcum + m_prev[..., None]
        m = jnp.maximum(m_inter, jnp.max(dmat, axis=-1))
        sc = jnp.einsum("bhtd,bhsd->bhts", qc, kc) * jnp.exp(dmat - m[..., None])
        inter = jnp.exp(m_inter - m)
        num = jnp.einsum("bhts,bhsv->bhtv", sc, vc) + inter[..., None] * jnp.einsum("bhtk,bhkv->bhtv", qc, c_prev)
        den = jnp.sum(sc, axis=-1) + inter * jnp.einsum("bhtk,bhk->bht", qc, n_prev)
        h = num / jnp.maximum(jnp.abs(den), jnp.exp(-m))[..., None]
        b_last = bcum[..., -1]
        g = b_last[..., None] - bcum + igc
        m_new = jnp.maximum(b_last + m_prev, jnp.max(g, axis=-1))
        wgt = jnp.exp(g - m_new[..., None])
        decay = jnp.exp(b_last + m_prev - m_new)
        c_new = decay[..., None, None] * c_prev + jnp.einsum("bhs,bhsk,bhsv->bhkv", wgt, kc, vc)
        n_new = decay[..., None] * n_prev + jnp.einsum("bhs,bhsk->bhk", wgt, kc)
        return (c_new, n_new, m_new), h

    init = (jnp.zeros((b, MLSTM_HEADS, MLSTM_DH, MLSTM_DH), jnp.float32),
            jnp.zeros((b, MLSTM_HEADS, MLSTM_DH), jnp.float32),
            jnp.zeros((b, MLSTM_HEADS), jnp.float32))
    _, hs = lax.scan(step, init, (qh, kh, vh, ig, lf))
    h = hs.transpose(1, 0, 3, 2, 4).reshape(b, s, MLSTM_HEADS, MLSTM_DH)
    h = h * lax.rsqrt(jnp.mean(h * h, axis=-1, keepdims=True) + EPS)
    h = h.reshape(b, s, GROUP) * norm_gain.astype(jnp.float32)
    return (jax.nn.sigmoid(o_pre.astype(jnp.float32)) * h).astype(dt)


def pool_mixer(u, w, scale):
    b, s, _ = u.shape
    uf = u.astype(jnp.float32).reshape(b, s, POOL_GROUPS, POOL_GC)
    cs0 = jnp.concatenate([jnp.zeros((b, 1, POOL_GROUPS, POOL_GC), jnp.float32),
                           jnp.cumsum(uf, axis=1)], axis=1)
    pos = jnp.arange(1, s + 1, dtype=jnp.float32)
    outs = []
    for g, win in enumerate(POOL_WINDOWS):
        hi = cs0[:, 1:, g]
        lo = jnp.concatenate([jnp.zeros((b, win - 1, POOL_GC), jnp.float32), cs0[:, :s + 1 - win, g]], axis=1)
        cnt = jnp.minimum(pos, float(win))[None, :, None]
        outs.append((hi - lo) / cnt - uf[:, :, g])
    p = jnp.stack(outs, axis=2)
    y = jnp.einsum("bsgc,gcd->bsgd", p, w.astype(jnp.float32)).reshape(b, s, GROUP)
    return (y * scale.astype(jnp.float32)).astype(u.dtype)


def conformer_conv(a, gt, conv_w, conv_b, ln_g, ln_b, pw_w, pw_b):
    u = a * jax.nn.sigmoid(gt)
    u = causal_dwconv(u, conv_w) + conv_b
    uf = u.astype(jnp.float32)
    mu = jnp.mean(uf, axis=-1, keepdims=True)
    var = jnp.mean(jnp.square(uf - mu), axis=-1, keepdims=True)
    uf = (uf - mu) * lax.rsqrt(var + EPS) * ln_g.astype(jnp.float32) + ln_b.astype(jnp.float32)
    u = jax.nn.silu(uf).astype(a.dtype)
    return u @ pw_w + pw_b


def short_gated_conv(bg, cg, xv, w):
    return bg * causal_dwconv(cg * xv, w)


def setup_inputs(seed: int = 0) -> dict:
    key = jax.random.key(seed)
    ks = jax.random.split(key, 20)
    n = jax.random.normal
    f32 = jnp.float32
    res_scale = (2 * DEPTH) ** -0.5
    return {
        "x": n(ks[0], (BATCH, SEQ, D_MODEL), f32),
        "attn_norm_gain": 1.0 + 0.02 * n(ks[1], (DEPTH, D_MODEL), f32),
        "w_in": n(ks[2], (DEPTH, D_MODEL, D_IN), f32) * D_MODEL ** -0.5,
        "mlstm_igate_bias": 0.1 * n(ks[3], (DEPTH, MLSTM_HEADS), f32),
        "mlstm_fgate_bias": 3.0 + 3.0 * jax.random.uniform(ks[4], (DEPTH, MLSTM_HEADS), f32),
        "mlstm_norm_gain": 1.0 + 0.02 * n(ks[5], (DEPTH, GROUP), f32),
        "pool_w": n(ks[6], (DEPTH, POOL_GROUPS, POOL_GC, POOL_GC), f32) * POOL_GC ** -0.5,
        "pool_scale": 1.0 + 0.1 * n(ks[7], (DEPTH, GROUP), f32),
        "conv_w": n(ks[8], (DEPTH, CONV_K, GROUP), f32) * CONV_K ** -0.5,
        "conv_b": 0.02 * n(ks[9], (DEPTH, GROUP), f32),
        "conv_ln_gain": 1.0 + 0.02 * n(ks[10], (DEPTH, GROUP), f32),
        "conv_ln_bias": 0.02 * n(ks[11], (DEPTH, GROUP), f32),
        "conv_pw_w": n(ks[12], (DEPTH, GROUP, GROUP), f32) * GROUP ** -0.5,
        "conv_pw_b": 0.02 * n(ks[13], (DEPTH, GROUP), f32),
        "sconv_w": n(ks[14], (DEPTH, SCONV_K, GROUP), f32) * SCONV_K ** -0.5,
        "w_out": n(ks[15], (DEPTH, D_MIX, D_MODEL), f32) * (D_MIX ** -0.5) * res_scale,
        "mlp_norm_gain": 1.0 + 0.02 * n(ks[16], (DEPTH, D_MODEL), f32),
        "w_up": n(ks[17], (DEPTH, D_MODEL, D_FF), f32) * D_MODEL ** -0.5,
        "w_down": n(ks[18], (DEPTH, D_FF, D_MODEL), f32) * (D_FF ** -0.5) * res_scale,
        "final_norm_gain": 1.0 + 0.02 * n(ks[19], (D_MODEL,), f32),
    }


def reference(x, attn_norm_gain, w_in, mlstm_igate_bias, mlstm_fgate_bias, mlstm_norm_gain,
              pool_w, pool_scale, conv_w, conv_b, conv_ln_gain, conv_ln_bias, conv_pw_w, conv_pw_b,
              sconv_w, w_out, mlp_norm_gain, w_up, w_down, final_norm_gain):
    for l in range(DEPTH):
        h = rmsnorm(x, attn_norm_gain[l])
        proj = h @ w_in[l]
        (q, k, v, o_pre, i_pre, f_pre, pool_in, glu_a, glu_g,
         sc_b, sc_c, sc_x) = jnp.split(proj, IN_SPLITS, axis=-1)
        y_a = mlstm_mixer(q, k, v, o_pre, i_pre, f_pre, mlstm_igate_bias[l], mlstm_fgate_bias[l],
                          mlstm_norm_gain[l])
        y_b = pool_mixer(pool_in, pool_w[l], pool_scale[l])
        y_c = conformer_conv(glu_a, glu_g, conv_w[l], conv_b[l], conv_ln_gain[l], conv_ln_bias[l],
                             conv_pw_w[l], conv_pw_b[l])
        y_d = short_gated_conv(sc_b, sc_c, sc_x, sconv_w[l])
        mixed = jnp.concatenate([y_a, y_b, y_c, y_d], axis=-1)
        x = x + mixed @ w_out[l]
        h = rmsnorm(x, mlp_norm_gain[l])
        x = x + jnp.square(jax.nn.relu(h @ w_up[l])) @ w_down[l]
    return rmsnorm(x, final_norm_gain)
```

```python
import functools

import jax
import jax.numpy as jnp
from jax import lax
from jax.experimental import pallas as pl
from jax.experimental.pallas import tpu as pltpu

EPS = 1e-6
HEADS = 4
POOL_WINDOWS = (2, 4, 8, 16)
CONV_K = 31
SCONV_K = 3
LANES = 128
SUBLANES = 8
SEQ_TILE = 256
MLP_TILE = 512
CONV_HALO = 32
POOL_HALO = 16
SCONV_HALO = 8
VMEM_LIMIT_BYTES = 56 * 1024 * 1024

f32 = jnp.float32
bf16 = jnp.bfloat16


def _dot(a, b):
    return jnp.dot(a, b, preferred_element_type=f32)


def _sigmoid(x):
    return 1.0 / (1.0 + jnp.exp(-x))


def _log_sigmoid(x):
    return jnp.minimum(x, 0.0) - jnp.log(1.0 + jnp.exp(-jnp.abs(x)))


def _lane_scan(x, combine, identity):
    n = x.shape[1]
    lane = lax.broadcasted_iota(jnp.int32, x.shape, 1)
    shift = 1
    while shift < n:
        shifted = pltpu.roll(x, shift, axis=1)
        x = combine(x, jnp.where(lane >= shift, shifted, identity))
        shift *= 2
    return x


def _expand_heads(col, group, width):
    t = col.shape[0]
    head = lax.broadcasted_iota(jnp.int32, (t, width), 1) // group
    out = jnp.broadcast_to(col[:, HEADS - 1:HEADS], (t, width))
    for h in range(HEADS - 2, -1, -1):
        out = jnp.where(head == h, jnp.broadcast_to(col[:, h:h + 1], (t, width)), out)
    return out


def _mixer_kernel(x_ref, ng_ref, win_ref, gbias_ref, mgain_ref, poolw_ref, pscale_ref,
                  convw_ref, convb_ref, lng_ref, lnb_ref, pww_ref, pwb_ref, scw_ref, wout_ref,
                  o_ref,
                  state_ref, mrow_ref, mlane_ref, ubuf_ref, pbuf_ref, sbuf_ref):
    t = x_ref.shape[1]
    g = mgain_ref.shape[1]
    dh = g // HEADS
    j = pl.program_id(1)

    @pl.when(j == 0)
    def _():
        state_ref[...] = jnp.zeros_like(state_ref)
        mrow_ref[...] = jnp.zeros_like(mrow_ref)
        mlane_ref[...] = jnp.zeros_like(mlane_ref)
        ubuf_ref[0:CONV_HALO, :] = jnp.zeros((CONV_HALO, g), f32)
        pbuf_ref[0:POOL_HALO, :] = jnp.zeros((POOL_HALO, g), f32)
        sbuf_ref[0:SCONV_HALO, :] = jnp.zeros((SCONV_HALO, g), f32)

    x = x_ref[0]
    hn = x * lax.rsqrt(jnp.mean(x * x, axis=-1, keepdims=True) + EPS) * ng_ref[...]
    hb = hn.astype(bf16)

    def proj(i, width=None):
        width = g if width is None else width
        return _dot(hb, win_ref[:, i * g:i * g + width])

    q = proj(0)
    k = proj(1) * (dh ** -0.5)
    v = proj(2)
    o_pre = proj(3)
    gates = proj(10, LANES) + gbias_ref[...]
    lane_g = lax.broadcasted_iota(jnp.int32, (t, LANES), 1)
    gcol = jnp.where(lane_g < HEADS, gates, _log_sigmoid(gates))
    grow = gcol.T[0:SUBLANES, :]
    bcum = _lane_scan(grow, jnp.add, 0.0)
    bc = pltpu.roll(bcum, HEADS, axis=0)
    a_row = grow - bc
    m_prev_row = jnp.concatenate([mrow_ref[...]] * (t // LANES), axis=1)
    m_row = jnp.maximum(_lane_scan(a_row, jnp.maximum, -jnp.inf), m_prev_row)
    sub8 = lax.broadcasted_iota(jnp.int32, (SUBLANES, t), 0)
    stacked = jnp.where(sub8 < HEADS, m_row, bcum)
    xt = jnp.concatenate([stacked, jnp.zeros((LANES - SUBLANES, t), f32)], axis=0).T
    bcol = pltpu.roll(xt, LANES - HEADS, axis=1)

    qb = q.astype(bf16)
    vb = v.astype(bf16)
    kt = k.T
    ktb = kt.astype(bf16)
    head_g = lax.broadcasted_iota(jnp.int32, (1, g), 1) // dh
    row_i = lax.broadcasted_iota(jnp.int32, (t, t), 0)
    col_i = lax.broadcasted_iota(jnp.int32, (t, t), 1)
    causal = col_i <= row_i
    lane_c = lax.broadcasted_iota(jnp.int32, (t, LANES), 1)
    num = jnp.zeros((t, g), f32)
    den_col = jnp.zeros((t, LANES), f32)
    for h in range(HEADS):
        hmask = head_g == h
        s_h = _dot(jnp.where(hmask, qb, jnp.zeros_like(qb)), ktb)
        e_h = jnp.exp(jnp.where(causal, a_row[h:h + 1, :] - xt[:, h:h + 1], -jnp.inf))
        p_h = s_h * e_h
        den_col = jnp.where(lane_c == h, jnp.sum(p_h, axis=1, keepdims=True), den_col)
        num = num + _dot(p_h.astype(bf16), jnp.where(hmask, vb, jnp.zeros_like(vb)))

    state = state_ref[...]
    qs = _dot(qb, state.astype(bf16))
    inter = jnp.exp(mlane_ref[...] - xt)
    den = den_col + inter * qs[:, g:]
    dmax = jnp.maximum(jnp.abs(den), jnp.exp(-(bcol + xt)))
    num = num + _expand_heads(inter, dh, g) * qs[:, :g]
    sq = num * num
    ss = jnp.zeros((t, LANES), f32)
    head_tg = lax.broadcasted_iota(jnp.int32, (t, g), 1) // dh
    for h in range(HEADS):
        ss = jnp.where(lane_c == h, jnp.sum(jnp.where(head_tg == h, sq, 0.0), axis=1, keepdims=True), ss)
    inv_d = 1.0 / dmax
    scale = inv_d * lax.rsqrt((ss * (1.0 / dh)) * inv_d * inv_d + EPS)
    y_a = _sigmoid(o_pre) * (num * _expand_heads(scale, dh, g) * mgain_ref[...])

    m_last = m_row[:, t - 1:t]
    wgt = jnp.exp(a_row - m_last)
    wexp = jnp.concatenate([jnp.broadcast_to(wgt[h:h + 1, :], (dh, t)) for h in range(HEADS)], axis=0)
    kwb = (kt * wexp).astype(bf16)
    v_aug = jnp.concatenate([vb, jnp.ones((t, LANES), bf16)], axis=1)
    upd = _dot(kwb, v_aug)
    x_last = xt[t - 1:t, :]
    decay_l = jnp.exp(mlane_ref[...] - x_last)
    decay = jnp.concatenate([_expand_heads(decay_l, dh, g), decay_l], axis=1)
    r_head = lax.broadcasted_iota(jnp.int32, (g, g + LANES), 0) // dh
    c_idx = lax.broadcasted_iota(jnp.int32, (g, g + LANES), 1)
    c_head = jnp.where(c_idx < g, c_idx // dh, c_idx - g)
    state_ref[...] = state * decay + jnp.where(r_head == c_head, upd, 0.0)
    mlane_ref[...] = x_last + pltpu.roll(x_last, LANES - HEADS, axis=1)
    mrow_ref[...] = jnp.broadcast_to((bc + m_row)[:, t - 1:t], (SUBLANES, LANES))

    pool_in = proj(4)
    pbuf_ref[POOL_HALO:POOL_HALO + t, :] = pool_in
    ext = pbuf_ref[...]
    win = []
    acc = ext
    shift = 1
    while shift < max(POOL_WINDOWS):
        acc = acc + pltpu.roll(acc, shift, axis=0)
        shift *= 2
        win.append(acc[POOL_HALO:POOL_HALO + t, :])
    gc = g // len(POOL_WINDOWS)
    grp = lax.broadcasted_iota(jnp.int32, (t, g), 1) // gc
    wsum = win[-1]
    wlen = jnp.full((t, g), float(POOL_WINDOWS[-1]), f32)
    for gi in range(len(POOL_WINDOWS) - 2, -1, -1):
        wsum = jnp.where(grp == gi, win[gi], wsum)
        wlen = jnp.where(grp == gi, float(POOL_WINDOWS[gi]), wlen)
    pos = (lax.broadcasted_iota(jnp.int32, (t, g), 0) + (j * t + 1)).astype(f32)
    pooled = wsum / jnp.minimum(pos, wlen) - pool_in
    y_b = _dot(pooled.astype(bf16), poolw_ref[...]) * pscale_ref[...]
    pbuf_ref[0:POOL_HALO, :] = pbuf_ref[t:t + POOL_HALO, :]

    glu = proj(5) * _sigmoid(proj(6))
    ubuf_ref[CONV_HALO:CONV_HALO + t, :] = glu
    cacc = jnp.broadcast_to(convb_ref[...], (t, g))
    off = CONV_HALO - (CONV_K - 1)
    for tap in range(CONV_K):
        cacc = cacc + convw_ref[tap:tap + 1, :] * ubuf_ref[off + tap:off + tap + t, :]
    ubuf_ref[0:CONV_HALO, :] = ubuf_ref[t:t + CONV_HALO, :]
    mu = jnp.mean(cacc, axis=-1, keepdims=True)
    cen = cacc - mu
    var = jnp.mean(cen * cen, axis=-1, keepdims=True)
    ln = cen * lax.rsqrt(var + EPS) * lng_ref[...] + lnb_ref[...]
    sw = ln * _sigmoid(ln)
    y_c = _dot(sw.astype(bf16), pww_ref[...]) + pwb_ref[...]

    sc_b = proj(7)
    cx = proj(8) * proj(9)
    sbuf_ref[SCONV_HALO:SCONV_HALO + t, :] = cx
    sacc = jnp.zeros((t, g), f32)
    soff = SCONV_HALO - (SCONV_K - 1)
    for tap in range(SCONV_K):
        sacc = sacc + scw_ref[tap:tap + 1, :] * sbuf_ref[soff + tap:soff + tap + t, :]
    sbuf_ref[0:SCONV_HALO, :] = sbuf_ref[t:t + SCONV_HALO, :]
    y_d = sc_b * sacc

    mixed = jnp.concatenate([y_a.astype(bf16), y_b.astype(bf16), y_c.astype(bf16), y_d.astype(bf16)], axis=1)
    o_ref[0] = x + _dot(mixed, wout_ref[...])


def _mlp_kernel(x_ref, ng_ref, wup_ref, wdown_ref, fg_ref, o_ref, *, final_norm):
    x = x_ref[...]
    hn = x * lax.rsqrt(jnp.mean(x * x, axis=-1, keepdims=True) + EPS) * ng_ref[...]
    up = _dot(hn.astype(bf16), wup_ref[...])
    act = jnp.maximum(up, 0.0)
    y = x + _dot((act * act).astype(bf16), wdown_ref[...])
    if final_norm:
        y = y * lax.rsqrt(jnp.mean(y * y, axis=-1, keepdims=True) + EPS) * fg_ref[...]
    o_ref[...] = y


def _const_spec(shape):
    return pl.BlockSpec(shape, lambda *_: (0,) * len(shape))


def _mixer_call(x, ng, win, gbias, mgain, poolw, pscale, convw, convb, lng, lnb, pww, pwb, scw, wout):
    b, s, d = x.shape
    g = mgain.shape[1]
    t = min(SEQ_TILE, s)
    assert s % t == 0 and t % LANES == 0
    small = [ng, win, gbias, mgain, poolw, pscale, convw, convb, lng, lnb, pww, pwb, scw, wout]
    return pl.pallas_call(
        _mixer_kernel,
        grid=(b, s // t),
        in_specs=[pl.BlockSpec((1, t, d), lambda i, j: (i, j, 0))] + [_const_spec(a.shape) for a in small],
        out_specs=pl.BlockSpec((1, t, d), lambda i, j: (i, j, 0)),
        out_shape=jax.ShapeDtypeStruct(x.shape, x.dtype),
        scratch_shapes=[
            pltpu.VMEM((g, g + LANES), f32),
            pltpu.VMEM((SUBLANES, LANES), f32),
            pltpu.VMEM((1, LANES), f32),
            pltpu.VMEM((CONV_HALO + t, g), f32),
            pltpu.VMEM((POOL_HALO + t, g), f32),
            pltpu.VMEM((SCONV_HALO + t, g), f32),
        ],
        compiler_params=pltpu.CompilerParams(
            dimension_semantics=("arbitrary", "arbitrary"), vmem_limit_bytes=VMEM_LIMIT_BYTES),
        name="mixer",
    )(x, *small)


def _mlp_call(x2, ng, wup, wdown, fg, final_norm):
    n, d = x2.shape
    tm = min(MLP_TILE, n)
    assert n % tm == 0
    return pl.pallas_call(
        functools.partial(_mlp_kernel, final_norm=final_norm),
        grid=(n // tm,),
        in_specs=[pl.BlockSpec((tm, d), lambda i: (i, 0)),
                  _const_spec(ng.shape),
                  pl.BlockSpec(wup.shape, lambda i: (0, 0), pipeline_mode=pl.Buffered(1)),
                  pl.BlockSpec(wdown.shape, lambda i: (0, 0), pipeline_mode=pl.Buffered(1)),
                  _const_spec(fg.shape)],
        out_specs=pl.BlockSpec((tm, d), lambda i: (i, 0)),
        out_shape=jax.ShapeDtypeStruct(x2.shape, x2.dtype),
        compiler_params=pltpu.CompilerParams(
            dimension_semantics=("arbitrary",), vmem_limit_bytes=VMEM_LIMIT_BYTES),
        name="mlp_final" if final_norm else "mlp",
    )(x2, ng, wup, wdown, fg)


def _block_diag(w):
    gn, c, _ = w.shape
    eye = jnp.eye(gn, dtype=w.dtype)
    return (eye[:, None, :, None] * w[:, :, None, :]).reshape(gn * c, gn * c)


def kernel(x, attn_norm_gain, w_in, mlstm_igate_bias, mlstm_fgate_bias, mlstm_norm_gain, pool_w, pool_scale, conv_w, conv_b, conv_ln_gain, conv_ln_bias, conv_pw_w, conv_pw_b, sconv_w, w_out, mlp_norm_gain, w_up, w_down, final_norm_gain):
    depth = w_in.shape[0]
    b, s, d = x.shape
    g = mlstm_norm_gain.shape[1]
    gate_lo = 4 * g
    gate_hi = gate_lo + 2 * HEADS
    row = lambda a: a[:, None, :]
    for l in range(depth):
        wl = w_in[l]
        win = jnp.concatenate(
            [wl[:, :gate_lo], wl[:, gate_hi:], wl[:, gate_lo:gate_hi],
             jnp.zeros((d, LANES - 2 * HEADS), wl.dtype)], axis=1).astype(bf16)
        gbias = jnp.concatenate(
            [mlstm_igate_bias[l], mlstm_fgate_bias[l], jnp.zeros((LANES - 2 * HEADS,), f32)])[None, :]
        x = _mixer_call(
            x, attn_norm_gain[l][None, :], win, gbias, mlstm_norm_gain[l][None, :],
            _block_diag(pool_w[l]).astype(bf16), pool_scale[l][None, :],
            conv_w[l], conv_b[l][None, :], conv_ln_gain[l][None, :], conv_ln_bias[l][None, :],
            conv_pw_w[l].astype(bf16), conv_pw_b[l][None, :], sconv_w[l], w_out[l].astype(bf16))
        x = _mlp_call(
            x.reshape(b * s, d), mlp_norm_gain[l][None, :], w_up[l].astype(bf16), w_down[l].astype(bf16),
            final_norm_gain[None, :], final_norm=(l == depth - 1)).reshape(b, s, d)
    return x
```

```python
import functools

import jax
import jax.numpy as jnp
from jax import lax
from jax.experimental import pallas as pl
from jax.experimental.pallas import tpu as pltpu

EPS = 1e-6
HEADS = 4
POOL_WINDOWS = (2, 4, 8, 16)
CONV_K = 31
SCONV_K = 3
LANES = 128
SUBLANES = 8
SEQ_TILE = 256
MLP_TILE = 512
CONV_HALO = 32
POOL_HALO = 16
SCONV_HALO = 8
CONV_ROWS = 32
VMEM_LIMIT_BYTES = 56 * 1024 * 1024

f32 = jnp.float32
bf16 = jnp.bfloat16


def _dot(a, b):
    return jnp.dot(a, b, preferred_element_type=f32)


def _sigmoid(x):
    return 1.0 / (1.0 + jnp.exp(-x))


def _log_sigmoid(x):
    return jnp.minimum(x, 0.0) - jnp.log(1.0 + jnp.exp(-jnp.abs(x)))


def _zero_after(x):
    bits = pltpu.bitcast(x[0:SUBLANES, :], jnp.int32)
    zero = lax.shift_right_logical(lax.shift_right_logical(bits, 16), 16)
    return zero[0:1, :].astype(f32)


def _split3(x):
    hi = x.astype(bf16).astype(f32)
    rest = x - hi
    mid = rest.astype(bf16).astype(f32)
    return hi, mid, rest - mid


def _expand_heads(col, group, width):
    t = col.shape[0]
    head = lax.broadcasted_iota(jnp.int32, (t, width), 1) // group
    out = jnp.broadcast_to(col[:, HEADS - 1:HEADS], (t, width))
    for h in range(HEADS - 2, -1, -1):
        out = jnp.where(head == h, jnp.broadcast_to(col[:, h:h + 1], (t, width)), out)
    return out


def _mixer_kernel(x_ref, ng_ref, win_ref, gbias_ref, mgain_ref, poolw_ref, pscale_ref,
                  convw_ref, convb_ref, lng_ref, lnb_ref, pww_ref, pwb_ref, scw_ref, wout_ref,
                  tri_ref, trit_ref,
                  o_ref,
                  state_ref, mlane_ref, ubuf_ref, cbuf_ref, pbuf_ref, sbuf_ref):
    t = x_ref.shape[1]
    g = mgain_ref.shape[1]
    dh = g // HEADS
    j = pl.program_id(1)

    @pl.when(j == 0)
    def _():
        state_ref[...] = jnp.zeros_like(state_ref)
        mlane_ref[...] = jnp.zeros_like(mlane_ref)
        ubuf_ref[0:CONV_HALO, :] = jnp.zeros((CONV_HALO, g), f32)
        pbuf_ref[0:POOL_HALO, :] = jnp.zeros((POOL_HALO, g), f32)
        sbuf_ref[0:SCONV_HALO, :] = jnp.zeros((SCONV_HALO, g), f32)

    x = x_ref[0]
    hn = x * lax.rsqrt(jnp.mean(x * x, axis=-1, keepdims=True) + EPS) * ng_ref[...]
    hb = hn.astype(bf16)

    def proj(i, width=None):
        width = g if width is None else width
        return _dot(hb, win_ref[:, i * g:i * g + width])

    off = CONV_HALO - (CONV_K - 1)
    tap_src = []
    for tap in range(CONV_K):
        b = (off + tap) % SUBLANES
        tap_src.append((ubuf_ref, off + tap) if b == 0 else (cbuf_ref.at[b - 1], off + tap - b))

    def conv_copies():
        for b in range(1, SUBLANES):
            rows = max(r for tap, (_, r) in enumerate(tap_src) if (off + tap) % SUBLANES == b) + t
            cbuf_ref[b - 1, 0:rows, :] = ubuf_ref[b:b + rows, :]

    def conv_block(r0, after):
        acc = jnp.broadcast_to(convb_ref[...] + _zero_after(after), (CONV_ROWS, g))
        for tap, (src, r) in enumerate(tap_src):
            acc = acc + convw_ref[tap:tap + 1, :] * src[r + r0:r + r0 + CONV_ROWS, :]
        return acc

    conv_rows = list(range(0, t, CONV_ROWS))
    per_dot = -(-len(conv_rows) // 4)
    blocks = []

    def conv_some(after):
        for r0 in conv_rows[len(blocks):len(blocks) + per_dot]:
            blocks.append(conv_block(r0, after))

    gates = proj(10, LANES) + gbias_ref[...]
    glu = proj(5) * _sigmoid(proj(6))
    ubuf_ref[CONV_HALO:CONV_HALO + t, :] = glu
    lane_c = lax.broadcasted_iota(jnp.int32, (t, LANES), 1)
    gcol = jnp.where(lane_c < HEADS, gates, _log_sigmoid(gates))
    grow = gcol.T[0:SUBLANES, :]
    rows3 = jnp.concatenate(list(_split3(grow)) + [jnp.zeros((SUBLANES, t), f32)], axis=0).astype(bf16)
    cols3 = jnp.concatenate(list(_split3(gcol)), axis=1).astype(bf16)
    pool_in = proj(4)
    pbuf_ref[POOL_HALO:POOL_HALO + t, :] = pool_in
    conv_copies()
    sc_c = proj(8)
    conv_some(sc_c)
    sc_x = proj(9)
    cx = sc_c * sc_x
    sbuf_ref[SCONV_HALO:SCONV_HALO + t, :] = cx
    conv_some(sc_x)
    sc_b = proj(7)
    conv_some(sc_b)
    k = proj(1) * (dh ** -0.5)
    conv_some(k)
    ubuf_ref[0:CONV_HALO, :] = ubuf_ref[t:t + CONV_HALO, :]
    q = proj(0)

    cacc = jnp.concatenate(blocks, axis=0)
    mu = jnp.mean(cacc, axis=-1, keepdims=True)
    cen = cacc - mu
    var = jnp.mean(cen * cen, axis=-1, keepdims=True)
    ln = cen * lax.rsqrt(var + EPS) * lng_ref[...] + lnb_ref[...]
    sw = ln * _sigmoid(ln)
    v = proj(2)

    ext = pbuf_ref[...]
    win = []
    acc = ext
    shift = 1
    while shift < max(POOL_WINDOWS):
        acc = acc + pltpu.roll(acc, shift, axis=0)
        shift *= 2
        win.append(acc[POOL_HALO:POOL_HALO + t, :])
    gc = g // len(POOL_WINDOWS)
    grp = lax.broadcasted_iota(jnp.int32, (t, g), 1) // gc
    wsum = win[-1]
    wlen = jnp.full((t, g), float(POOL_WINDOWS[-1]), f32)
    for gi in range(len(POOL_WINDOWS) - 2, -1, -1):
        wsum = jnp.where(grp == gi, win[gi], wsum)
        wlen = jnp.where(grp == gi, float(POOL_WINDOWS[gi]), wlen)
    pos = (lax.broadcasted_iota(jnp.int32, (t, g), 0) + (j * t + 1)).astype(f32)
    pooled = wsum / jnp.minimum(pos, wlen) - pool_in
    pbuf_ref[0:POOL_HALO, :] = pbuf_ref[t:t + POOL_HALO, :]
    o_pre = proj(3)

    sacc = jnp.zeros((t, g), f32)
    soff = SCONV_HALO - (SCONV_K - 1)
    for tap in range(SCONV_K):
        sacc = sacc + scw_ref[tap:tap + 1, :] * sbuf_ref[soff + tap:soff + tap + t, :]
    sbuf_ref[0:SCONV_HALO, :] = sbuf_ref[t:t + SCONV_HALO, :]
    y_d = sc_b * sacc

    qb = q.astype(bf16)
    vb = v.astype(bf16)
    kt = k.T
    ktb = kt.astype(bf16)
    head_g = lax.broadcasted_iota(jnp.int32, (1, g), 1) // dh
    scores = []
    for h in range(HEADS):
        parts = [jnp.zeros((h * dh, t), bf16), ktb[h * dh:(h + 1) * dh, :], jnp.zeros((g - (h + 1) * dh, t), bf16)]
        scores.append(_dot(qb, jnp.concatenate([p for p in parts if p.shape[0]], axis=0)))
    state = state_ref[...]
    qs = _dot(qb, state.astype(bf16))
    rsum = _dot(rows3, trit_ref[...])
    bcum_row = rsum[0:SUBLANES] + rsum[SUBLANES:2 * SUBLANES] + rsum[2 * SUBLANES:3 * SUBLANES]
    a_row = grow - pltpu.roll(bcum_row, HEADS, axis=0)
    csum = _dot(tri_ref[...], cols3)
    bcum_col = csum[:, 0:LANES] + csum[:, LANES:2 * LANES] + csum[:, 2 * LANES:3 * LANES]
    bcol = pltpu.roll(bcum_col, LANES - HEADS, axis=1)

    row_i = lax.broadcasted_iota(jnp.int32, (t, t), 0)
    col_i = lax.broadcasted_iota(jnp.int32, (t, t), 1)
    causal = col_i <= row_i
    m_prev = mlane_ref[...]
    num = jnp.zeros((t, g), f32)
    den_col = jnp.zeros((t, LANES), f32)
    mcol = jnp.zeros((t, LANES), f32)
    wgt_rows = []
    for h in range(HEADS):
        a_h = a_row[h:h + 1, :]
        masked = jnp.where(causal, a_h, -jnp.inf)
        m_h = jnp.maximum(jnp.max(masked, axis=1, keepdims=True), m_prev[:, h:h + 1])
        p_h = scores[h] * jnp.exp(masked - m_h)
        den_col = jnp.where(lane_c == h, jnp.sum(p_h, axis=1, keepdims=True), den_col)
        mcol = jnp.where(lane_c == h, m_h, mcol)
        num = num + _dot(p_h.astype(bf16), jnp.where(head_g == h, vb, jnp.zeros_like(vb)))
        wgt_rows.append(jnp.exp(a_h - m_h[t - 1:t, :]))

    wexp = jnp.concatenate([jnp.broadcast_to(w, (dh, t)) for w in wgt_rows], axis=0)
    kwb = (kt * wexp).astype(bf16)
    v_aug = jnp.concatenate([vb, jnp.ones((t, LANES), bf16)], axis=1)
    upd = _dot(kwb, v_aug)
    m_last = mcol[t - 1:t, :]
    decay_l = jnp.exp(m_prev - m_last)
    decay = jnp.concatenate([_expand_heads(decay_l, dh, g), decay_l], axis=1)
    r_head = lax.broadcasted_iota(jnp.int32, (g, g + LANES), 0) // dh
    c_idx = lax.broadcasted_iota(jnp.int32, (g, g + LANES), 1)
    c_head = jnp.where(c_idx < g, c_idx // dh, c_idx - g)
    state_ref[...] = state * decay + jnp.where(r_head == c_head, upd, 0.0)
    mlane_ref[...] = bcol[t - 1:t, :] + m_last
    y_b = _dot(pooled.astype(bf16), poolw_ref[...]) * pscale_ref[...]
    y_c = _dot(sw.astype(bf16), pww_ref[...]) + pwb_ref[...]

    inter = jnp.exp(m_prev - mcol)
    den = den_col + inter * qs[:, g:]
    dmax = jnp.maximum(jnp.abs(den), jnp.exp(-(bcol + mcol)))
    num = num + _expand_heads(inter, dh, g) * qs[:, :g]
    sq = num * num
    ss = jnp.zeros((t, LANES), f32)
    head_tg = lax.broadcasted_iota(jnp.int32, (t, g), 1) // dh
    for h in range(HEADS):
        ss = jnp.where(lane_c == h, jnp.sum(jnp.where(head_tg == h, sq, 0.0), axis=1, keepdims=True), ss)
    inv_d = 1.0 / dmax
    scale = inv_d * lax.rsqrt((ss * (1.0 / dh)) * inv_d * inv_d + EPS)
    y_a = _sigmoid(o_pre) * (num * _expand_heads(scale, dh, g) * mgain_ref[...])

    mixed = jnp.concatenate([y_a.astype(bf16), y_b.astype(bf16), y_c.astype(bf16), y_d.astype(bf16)], axis=1)
    o_ref[0] = x + _dot(mixed, wout_ref[...])


def _mlp_kernel(x_ref, ng_ref, wup_ref, wdown_ref, fg_ref, o_ref, *, final_norm):
    x = x_ref[...]
    hn = x * lax.rsqrt(jnp.mean(x * x, axis=-1, keepdims=True) + EPS) * ng_ref[...]
    up = _dot(hn.astype(bf16), wup_ref[...])
    act = jnp.maximum(up, 0.0)
    y = x + _dot((act * act).astype(bf16), wdown_ref[...])
    if final_norm:
        y = y * lax.rsqrt(jnp.mean(y * y, axis=-1, keepdims=True) + EPS) * fg_ref[...]
    o_ref[...] = y


def _const_spec(shape):
    return pl.BlockSpec(shape, lambda *_: (0,) * len(shape))


def _mixer_call(x, ng, win, gbias, mgain, poolw, pscale, convw, convb, lng, lnb, pww, pwb, scw, wout):
    b, s, d = x.shape
    g = mgain.shape[1]
    t = min(SEQ_TILE, s)
    assert s % t == 0 and t % LANES == 0
    tri = jnp.tril(jnp.ones((t, t), bf16))
    small = [ng, win, gbias, mgain, poolw, pscale, convw, convb, lng, lnb, pww, pwb, scw, wout, tri, tri.T]
    return pl.pallas_call(
        _mixer_kernel,
        grid=(b, s // t),
        in_specs=[pl.BlockSpec((1, t, d), lambda i, j: (i, j, 0))] + [_const_spec(a.shape) for a in small],
        out_specs=pl.BlockSpec((1, t, d), lambda i, j: (i, j, 0)),
        out_shape=jax.ShapeDtypeStruct(x.shape, x.dtype),
        scratch_shapes=[
            pltpu.VMEM((g, g + LANES), f32),
            pltpu.VMEM((1, LANES), f32),
            pltpu.VMEM((CONV_HALO + t, g), f32),
            pltpu.VMEM((SUBLANES - 1, CONV_HALO + t, g), f32),
            pltpu.VMEM((POOL_HALO + t, g), f32),
            pltpu.VMEM((SCONV_HALO + t, g), f32),
        ],
        compiler_params=pltpu.CompilerParams(
            dimension_semantics=("arbitrary", "arbitrary"), vmem_limit_bytes=VMEM_LIMIT_BYTES),
        name="mixer",
    )(x, *small)


def _mlp_call(x2, ng, wup, wdown, fg, final_norm):
    n, d = x2.shape
    tm = min(MLP_TILE, n)
    assert n % tm == 0
    return pl.pallas_call(
        functools.partial(_mlp_kernel, final_norm=final_norm),
        grid=(n // tm,),
        in_specs=[pl.BlockSpec((tm, d), lambda i: (i, 0)),
                  _const_spec(ng.shape),
                  pl.BlockSpec(wup.shape, lambda i: (0, 0), pipeline_mode=pl.Buffered(1)),
                  pl.BlockSpec(wdown.shape, lambda i: (0, 0), pipeline_mode=pl.Buffered(1)),
                  _const_spec(fg.shape)],
        out_specs=pl.BlockSpec((tm, d), lambda i: (i, 0)),
        out_shape=jax.ShapeDtypeStruct(x2.shape, x2.dtype),
        compiler_params=pltpu.CompilerParams(
            dimension_semantics=("arbitrary",), vmem_limit_bytes=VMEM_LIMIT_BYTES),
        name="mlp_final" if final_norm else "mlp",
    )(x2, ng, wup, wdown, fg)


def _block_diag(w):
    gn, c, _ = w.shape
    eye = jnp.eye(gn, dtype=w.dtype)
    return (eye[:, None, :, None] * w[:, :, None, :]).reshape(gn * c, gn * c)


def kernel(x, attn_norm_gain, w_in, mlstm_igate_bias, mlstm_fgate_bias, mlstm_norm_gain, pool_w, pool_scale, conv_w, conv_b, conv_ln_gain, conv_ln_bias, conv_pw_w, conv_pw_b, sconv_w, w_out, mlp_norm_gain, w_up, w_down, final_norm_gain):
    depth = w_in.shape[0]
    b, s, d = x.shape
    g = mlstm_norm_gain.shape[1]
    gate_lo = 4 * g
    gate_hi = gate_lo + 2 * HEADS
    for l in range(depth):
        wl = w_in[l]
        win = jnp.concatenate(
            [wl[:, :gate_lo], wl[:, gate_hi:], wl[:, gate_lo:gate_hi],
             jnp.zeros((d, LANES - 2 * HEADS), wl.dtype)], axis=1).astype(bf16)
        gbias = jnp.concatenate(
            [mlstm_igate_bias[l], mlstm_fgate_bias[l], jnp.zeros((LANES - 2 * HEADS,), f32)])[None, :]
        x = _mixer_call(
            x, attn_norm_gain[l][None, :], win, gbias, mlstm_norm_gain[l][None, :],
            _block_diag(pool_w[l]).astype(bf16), pool_scale[l][None, :],
            conv_w[l], conv_b[l][None, :], conv_ln_gain[l][None, :], conv_ln_bias[l][None, :],
            conv_pw_w[l].astype(bf16), conv_pw_b[l][None, :], sconv_w[l], w_out[l].astype(bf16))
        x = _mlp_call(
            x.reshape(b * s, d), mlp_norm_gain[l][None, :], w_up[l].astype(bf16), w_down[l].astype(bf16),
            final_norm_gain[None, :], final_norm=(l == depth - 1)).reshape(b, s, d)
    return x
```

```python
import functools

import jax
import jax.numpy as jnp
from jax import lax
from jax.experimental import pallas as pl
from jax.experimental.pallas import tpu as pltpu

EPS = 1e-6
HEADS = 4
POOL_WINDOWS = (2, 4, 8, 16)
CONV_K = 31
SCONV_K = 3
LANES = 128
SUBLANES = 8
SEQ_TILE = 256
BIG_OPERAND_BYTES = 1 << 20
CONV_HALO = 32
POOL_HALO = 16
SCONV_HALO = 8
CONV_ROWS = 32
VMEM_LIMIT_BYTES = 56 * 1024 * 1024

f32 = jnp.float32
bf16 = jnp.bfloat16


def _dot(a, b):
    return jnp.dot(a, b, preferred_element_type=f32)


def _sigmoid(x):
    return 1.0 / (1.0 + jnp.exp(-x))


def _log_sigmoid(x):
    return jnp.minimum(x, 0.0) - jnp.log(1.0 + jnp.exp(-jnp.abs(x)))


def _split3(x):
    hi = x.astype(bf16).astype(f32)
    rest = x - hi
    mid = rest.astype(bf16).astype(f32)
    return hi, mid, rest - mid


def _expand_heads(col, group, width):
    t = col.shape[0]
    head = lax.broadcasted_iota(jnp.int32, (t, width), 1) // group
    out = jnp.broadcast_to(col[:, HEADS - 1:HEADS], (t, width))
    for h in range(HEADS - 2, -1, -1):
        out = jnp.where(head == h, jnp.broadcast_to(col[:, h:h + 1], (t, width)), out)
    return out


def _layer_kernel(x_ref, ng_ref, win_ref, gbias_ref, mgain_ref, poolw_ref, pscale_ref,
                  convw_ref, convb_ref, lng_ref, lnb_ref, pww_ref, pwb_ref, scw_ref, wout_ref,
                  tri_ref, trit_ref, mlpg_ref, wup_ref, wdown_ref, fg_ref,
                  o_ref,
                  hand_ref, state_ref, mlane_ref, ubuf_ref, cbuf_ref, pbuf_ref, sbuf_ref,
                  *, tiles_per_seq, final_norm):
    t = x_ref.shape[0]
    g = mgain_ref.shape[1]
    dh = g // HEADS
    step = pl.program_id(0)
    j = lax.rem(step, tiles_per_seq)

    @pl.when(step == 0)
    def _():
        hand_ref[...] = jnp.zeros_like(hand_ref)

    @pl.when(j == 0)
    def _():
        state_ref[...] = jnp.zeros_like(state_ref)
        mlane_ref[...] = jnp.zeros_like(mlane_ref)
        ubuf_ref[0:CONV_HALO, :] = jnp.zeros((CONV_HALO, g), f32)
        pbuf_ref[0:POOL_HALO, :] = jnp.zeros((POOL_HALO, g), f32)
        sbuf_ref[0:SCONV_HALO, :] = jnp.zeros((SCONV_HALO, g), f32)

    x1 = hand_ref[...]
    h1b = (x1 * lax.rsqrt(jnp.mean(x1 * x1, axis=-1, keepdims=True) + EPS) * mlpg_ref[...]).astype(bf16)

    x = x_ref[...]
    hn = x * lax.rsqrt(jnp.mean(x * x, axis=-1, keepdims=True) + EPS) * ng_ref[...]
    hb = hn.astype(bf16)

    def proj(i, width=None):
        width = g if width is None else width
        return _dot(hb, win_ref[:, i * g:i * g + width])

    off = CONV_HALO - (CONV_K - 1)
    tap_src = []
    for tap in range(CONV_K):
        b = (off + tap) % SUBLANES
        tap_src.append((ubuf_ref, off + tap) if b == 0 else (cbuf_ref.at[b - 1], off + tap - b))

    def conv_copies():
        for b in range(1, SUBLANES):
            rows = max(r for tap, (_, r) in enumerate(tap_src) if (off + tap) % SUBLANES == b) + t
            cbuf_ref[b - 1, 0:rows, :] = ubuf_ref[b:b + rows, :]

    def conv_block(r0):
        acc = jnp.broadcast_to(convb_ref[...], (CONV_ROWS, g))
        for tap, (src, r) in enumerate(tap_src):
            acc = acc + convw_ref[tap:tap + 1, :] * src[r + r0:r + r0 + CONV_ROWS, :]
        return acc

    gates = proj(10, LANES) + gbias_ref[...]
    glu = proj(5) * _sigmoid(proj(6))
    ubuf_ref[CONV_HALO:CONV_HALO + t, :] = glu
    lane_c = lax.broadcasted_iota(jnp.int32, (t, LANES), 1)
    gcol = jnp.where(lane_c < HEADS, gates, _log_sigmoid(gates))
    grow = gcol.T[0:SUBLANES, :]
    rows3 = jnp.concatenate(list(_split3(grow)) + [jnp.zeros((SUBLANES, t), f32)], axis=0).astype(bf16)
    cols3 = jnp.concatenate(list(_split3(gcol)), axis=1).astype(bf16)
    pool_in = proj(4)
    pbuf_ref[POOL_HALO:POOL_HALO + t, :] = pool_in
    cx = proj(8) * proj(9)
    sbuf_ref[SCONV_HALO:SCONV_HALO + t, :] = cx
    sc_b = proj(7)
    k = proj(1) * (dh ** -0.5)
    q = proj(0)

    conv_copies()
    cacc = jnp.concatenate([conv_block(r0) for r0 in range(0, t, CONV_ROWS)], axis=0)
    ubuf_ref[0:CONV_HALO, :] = ubuf_ref[t:t + CONV_HALO, :]
    mu = jnp.mean(cacc, axis=-1, keepdims=True)
    cen = cacc - mu
    var = jnp.mean(cen * cen, axis=-1, keepdims=True)
    ln = cen * lax.rsqrt(var + EPS) * lng_ref[...] + lnb_ref[...]
    sw = ln * _sigmoid(ln)
    v = proj(2)

    ext = pbuf_ref[...]
    win = []
    acc = ext
    shift = 1
    while shift < max(POOL_WINDOWS):
        acc = acc + pltpu.roll(acc, shift, axis=0)
        shift *= 2
        win.append(acc[POOL_HALO:POOL_HALO + t, :])
    gc = g // len(POOL_WINDOWS)
    grp = lax.broadcasted_iota(jnp.int32, (t, g), 1) // gc
    wsum = win[-1]
    wlen = jnp.full((t, g), float(POOL_WINDOWS[-1]), f32)
    for gi in range(len(POOL_WINDOWS) - 2, -1, -1):
        wsum = jnp.where(grp == gi, win[gi], wsum)
        wlen = jnp.where(grp == gi, float(POOL_WINDOWS[gi]), wlen)
    pos = (lax.broadcasted_iota(jnp.int32, (t, g), 0) + (j * t + 1)).astype(f32)
    pooled = wsum / jnp.minimum(pos, wlen) - pool_in
    pbuf_ref[0:POOL_HALO, :] = pbuf_ref[t:t + POOL_HALO, :]
    o_pre = proj(3)

    sacc = jnp.zeros((t, g), f32)
    soff = SCONV_HALO - (SCONV_K - 1)
    for tap in range(SCONV_K):
        sacc = sacc + scw_ref[tap:tap + 1, :] * sbuf_ref[soff + tap:soff + tap + t, :]
    sbuf_ref[0:SCONV_HALO, :] = sbuf_ref[t:t + SCONV_HALO, :]
    y_d = sc_b * sacc

    qb = q.astype(bf16)
    vb = v.astype(bf16)
    kt = k.T
    ktb = kt.astype(bf16)
    head_g = lax.broadcasted_iota(jnp.int32, (1, g), 1) // dh
    scores = []
    for h in range(HEADS):
        parts = [jnp.zeros((h * dh, t), bf16), ktb[h * dh:(h + 1) * dh, :], jnp.zeros((g - (h + 1) * dh, t), bf16)]
        scores.append(_dot(qb, jnp.concatenate([p for p in parts if p.shape[0]], axis=0)))
    state = state_ref[...]
    qs = _dot(qb, state.astype(bf16))
    rsum = _dot(rows3, trit_ref[...])
    bcum_row = rsum[0:SUBLANES] + rsum[SUBLANES:2 * SUBLANES] + rsum[2 * SUBLANES:3 * SUBLANES]
    a_row = grow - pltpu.roll(bcum_row, HEADS, axis=0)
    csum = _dot(tri_ref[...], cols3)
    bcum_col = csum[:, 0:LANES] + csum[:, LANES:2 * LANES] + csum[:, 2 * LANES:3 * LANES]
    bcol = pltpu.roll(bcum_col, LANES - HEADS, axis=1)

    act = jnp.maximum(_dot(h1b, wup_ref[...]), 0.0)
    actb = (act * act).astype(bf16)

    row_i = lax.broadcasted_iota(jnp.int32, (t, t), 0)
    col_i = lax.broadcasted_iota(jnp.int32, (t, t), 1)
    causal = col_i <= row_i
    m_prev = mlane_ref[...]
    num = jnp.zeros((t, g), f32)
    den_col = jnp.zeros((t, LANES), f32)
    mcol = jnp.zeros((t, LANES), f32)
    wgt_rows = []
    for h in range(HEADS):
        a_h = a_row[h:h + 1, :]
        masked = jnp.where(causal, a_h, -jnp.inf)
        m_h = jnp.maximum(jnp.max(masked, axis=1, keepdims=True), m_prev[:, h:h + 1])
        p_h = scores[h] * jnp.exp(masked - m_h)
        den_col = jnp.where(lane_c == h, jnp.sum(p_h, axis=1, keepdims=True), den_col)
        mcol = jnp.where(lane_c == h, m_h, mcol)
        num = num + _dot(p_h.astype(bf16), jnp.where(head_g == h, vb, jnp.zeros_like(vb)))
        wgt_rows.append(jnp.exp(a_h - m_h[t - 1:t, :]))

    wexp = jnp.concatenate([jnp.broadcast_to(w, (dh, t)) for w in wgt_rows], axis=0)
    kwb = (kt * wexp).astype(bf16)
    v_aug = jnp.concatenate([vb, jnp.ones((t, LANES), bf16)], axis=1)
    upd = _dot(kwb, v_aug)
    m_last = mcol[t - 1:t, :]
    decay_l = jnp.exp(m_prev - m_last)
    decay = jnp.concatenate([_expand_heads(decay_l, dh, g), decay_l], axis=1)
    r_head = lax.broadcasted_iota(jnp.int32, (g, g + LANES), 0) // dh
    c_idx = lax.broadcasted_iota(jnp.int32, (g, g + LANES), 1)
    c_head = jnp.where(c_idx < g, c_idx // dh, c_idx - g)
    state_ref[...] = state * decay + jnp.where(r_head == c_head, upd, 0.0)
    mlane_ref[...] = bcol[t - 1:t, :] + m_last
    y_b = _dot(pooled.astype(bf16), poolw_ref[...]) * pscale_ref[...]
    y_c = _dot(sw.astype(bf16), pww_ref[...]) + pwb_ref[...]

    y = x1 + _dot(actb, wdown_ref[...])
    if final_norm:
        y = y * lax.rsqrt(jnp.mean(y * y, axis=-1, keepdims=True) + EPS) * fg_ref[...]
    o_ref[...] = y

    inter = jnp.exp(m_prev - mcol)
    den = den_col + inter * qs[:, g:]
    dmax = jnp.maximum(jnp.abs(den), jnp.exp(-(bcol + mcol)))
    num = num + _expand_heads(inter, dh, g) * qs[:, :g]
    sq = num * num
    ss = jnp.zeros((t, LANES), f32)
    head_tg = lax.broadcasted_iota(jnp.int32, (t, g), 1) // dh
    for h in range(HEADS):
        ss = jnp.where(lane_c == h, jnp.sum(jnp.where(head_tg == h, sq, 0.0), axis=1, keepdims=True), ss)
    inv_d = 1.0 / dmax
    scale = inv_d * lax.rsqrt((ss * (1.0 / dh)) * inv_d * inv_d + EPS)
    y_a = _sigmoid(o_pre) * (num * _expand_heads(scale, dh, g) * mgain_ref[...])

    mixed = jnp.concatenate([y_a.astype(bf16), y_b.astype(bf16), y_c.astype(bf16), y_d.astype(bf16)], axis=1)
    hand_ref[...] = x + _dot(mixed, wout_ref[...])


def _const_spec(shape, single_buffer=False):
    index_map = lambda *_: (0,) * len(shape)
    if single_buffer:
        return pl.BlockSpec(shape, index_map, pipeline_mode=pl.Buffered(1))
    return pl.BlockSpec(shape, index_map)


def _layer_call(x2, seq_len, mixer_params, mlp_params, final_norm):
    n, d = x2.shape
    g = mixer_params[3].shape[1]
    t = min(SEQ_TILE, seq_len)
    assert seq_len % t == 0 and t % LANES == 0
    tiles = n // t
    tri = jnp.tril(jnp.ones((t, t), bf16))
    consts = list(mixer_params) + [tri, tri.T] + list(mlp_params)
    return pl.pallas_call(
        functools.partial(_layer_kernel, tiles_per_seq=seq_len // t, final_norm=final_norm),
        grid=(tiles + 1,),
        in_specs=[pl.BlockSpec((t, d), lambda s: (jnp.minimum(s, tiles - 1), 0))]
        + [_const_spec(a.shape, single_buffer=a.size * a.dtype.itemsize >= BIG_OPERAND_BYTES) for a in consts],
        out_specs=pl.BlockSpec((t, d), lambda s: (jnp.maximum(s - 1, 0), 0)),
        out_shape=jax.ShapeDtypeStruct(x2.shape, x2.dtype),
        scratch_shapes=[
            pltpu.VMEM((t, d), f32),
            pltpu.VMEM((g, g + LANES), f32),
            pltpu.VMEM((1, LANES), f32),
            pltpu.VMEM((CONV_HALO + t, g), f32),
            pltpu.VMEM((SUBLANES - 1, CONV_HALO + t, g), f32),
            pltpu.VMEM((POOL_HALO + t, g), f32),
            pltpu.VMEM((SCONV_HALO + t, g), f32),
        ],
        compiler_params=pltpu.CompilerParams(
            dimension_semantics=("arbitrary",), vmem_limit_bytes=VMEM_LIMIT_BYTES),
        name="layer_final" if final_norm else "layer",
    )(x2, *consts)


def _block_diag(w):
    gn, c, _ = w.shape
    eye = jnp.eye(gn, dtype=w.dtype)
    return (eye[:, None, :, None] * w[:, :, None, :]).reshape(gn * c, gn * c)


def kernel(x, attn_norm_gain, w_in, mlstm_igate_bias, mlstm_fgate_bias, mlstm_norm_gain, pool_w, pool_scale, conv_w, conv_b, conv_ln_gain, conv_ln_bias, conv_pw_w, conv_pw_b, sconv_w, w_out, mlp_norm_gain, w_up, w_down, final_norm_gain):
    depth = w_in.shape[0]
    b, s, d = x.shape
    g = mlstm_norm_gain.shape[1]
    gate_lo = 4 * g
    gate_hi = gate_lo + 2 * HEADS
    x2 = x.reshape(b * s, d)
    for l in range(depth):
        wl = w_in[l]
        win = jnp.concatenate(
            [wl[:, :gate_lo], wl[:, gate_hi:], wl[:, gate_lo:gate_hi],
             jnp.zeros((d, LANES - 2 * HEADS), wl.dtype)], axis=1).astype(bf16)
        gbias = jnp.concatenate(
            [mlstm_igate_bias[l], mlstm_fgate_bias[l], jnp.zeros((LANES - 2 * HEADS,), f32)])[None, :]
        mixer_params = (
            attn_norm_gain[l][None, :], win, gbias, mlstm_norm_gain[l][None, :],
            _block_diag(pool_w[l]).astype(bf16), pool_scale[l][None, :],
            conv_w[l], conv_b[l][None, :], conv_ln_gain[l][None, :], conv_ln_bias[l][None, :],
            conv_pw_w[l].astype(bf16), conv_pw_b[l][None, :], sconv_w[l], w_out[l].astype(bf16))
        mlp_params = (mlp_norm_gain[l][None, :], w_up[l].astype(bf16), w_down[l].astype(bf16),
                      final_norm_gain[None, :])
        x2 = _layer_call(x2, s, mixer_params, mlp_params, final_norm=(l == depth - 1))
    return x2.reshape(b, s, d)
```

```python
import functools

import jax
import jax.numpy as jnp
from jax import lax
from jax.experimental import pallas as pl
from jax.experimental.pallas import tpu as pltpu

EPS = 1e-6
HEADS = 4
POOL_WINDOWS = (2, 4, 8, 16)
assert all(w & (w - 1) == 0 for w in POOL_WINDOWS)
CONV_K = 31
SCONV_K = 3
LANES = 128
SUBLANES = 8
SEQ_TILE = 256
BIG_OPERAND_BYTES = 1 << 20
CONV_HALO = 32
POOL_HALO = 16
SCONV_HALO = 8
CONV_ROWS = 32
VMEM_LIMIT_BYTES = 56 * 1024 * 1024

f32 = jnp.float32
bf16 = jnp.bfloat16


def _dot(a, b):
    return jnp.dot(a, b, preferred_element_type=f32)


def _sigmoid(x):
    return 1.0 / (1.0 + jnp.exp(-x))


def _log_sigmoid(x):
    return jnp.minimum(x, 0.0) - jnp.log(1.0 + jnp.exp(-jnp.abs(x)))


def _split3(x):
    hi = x.astype(bf16).astype(f32)
    rest = x - hi
    mid = rest.astype(bf16).astype(f32)
    return hi, mid, rest - mid


def _expand_heads(col, group, width):
    t = col.shape[0]
    head = lax.broadcasted_iota(jnp.int32, (t, width), 1) // group
    out = jnp.broadcast_to(col[:, HEADS - 1:HEADS], (t, width))
    for h in range(HEADS - 2, -1, -1):
        out = jnp.where(head == h, jnp.broadcast_to(col[:, h:h + 1], (t, width)), out)
    return out


def _layer_kernel(x_ref, trit_ref, fg_ref, ng_ref, wqkvo_ref, wrest_ref, wgate_ref, gbias_ref, mgain_ref,
                  poolw_ref, pscale_ref, convw_ref, convb_ref, lng_ref, lnb_ref, pww_ref, pwb_ref, scw_ref,
                  wout_ref, mlpg_ref, wup_ref, wdown_ref,
                  o_ref,
                  hand_ref, state_ref, mlane_ref, ubuf_ref, cbuf_ref, pbuf_ref, sbuf_ref,
                  *, tiles_per_seq, final_norm):
    t = x_ref.shape[0]
    g = mgain_ref.shape[1]
    dh = g // HEADS
    step = pl.program_id(0)
    j = lax.rem(step, tiles_per_seq)

    @pl.when(step == 0)
    def _():
        hand_ref[...] = jnp.zeros_like(hand_ref)

    @pl.when(j == 0)
    def _():
        state_ref[...] = jnp.zeros_like(state_ref)
        mlane_ref[...] = jnp.zeros_like(mlane_ref)
        ubuf_ref[0:CONV_HALO, :] = jnp.zeros((CONV_HALO, g), f32)
        pbuf_ref[0:POOL_HALO, :] = jnp.zeros((POOL_HALO, g), f32)
        sbuf_ref[0:SCONV_HALO, :] = jnp.zeros((SCONV_HALO, g), f32)

    x = x_ref[...]
    hn = x * lax.rsqrt(jnp.mean(x * x, axis=-1, keepdims=True) + EPS) * ng_ref[...]
    hb = hn.astype(bf16)

    def proj(i):
        if i < 4:
            return _dot(hb, wqkvo_ref[:, i * g:(i + 1) * g])
        return _dot(hb, wrest_ref[:, (i - 4) * g:(i - 3) * g])

    off = CONV_HALO - (CONV_K - 1)
    tap_src = []
    for tap in range(CONV_K):
        b = (off + tap) % SUBLANES
        tap_src.append((ubuf_ref, off + tap) if b == 0 else (cbuf_ref.at[b - 1], off + tap - b))

    def conv_copies():
        for b in range(1, SUBLANES):
            rows = max(r for tap, (_, r) in enumerate(tap_src) if (off + tap) % SUBLANES == b) + t
            cbuf_ref[b - 1, 0:rows, :] = ubuf_ref[b:b + rows, :]

    def conv_block(r0):
        acc = jnp.broadcast_to(convb_ref[...], (CONV_ROWS, g))
        for tap, (src, r) in enumerate(tap_src):
            acc = acc + convw_ref[tap:tap + 1, :] * src[r + r0:r + r0 + CONV_ROWS, :]
        return acc

    gates = _dot(hb, wgate_ref[...]) + gbias_ref[...]

    x1 = hand_ref[...]
    h1b = (x1 * lax.rsqrt(jnp.mean(x1 * x1, axis=-1, keepdims=True) + EPS) * mlpg_ref[...]).astype(bf16)

    glu = proj(5) * _sigmoid(proj(6))
    ubuf_ref[CONV_HALO:CONV_HALO + t, :] = glu
    lane_c = lax.broadcasted_iota(jnp.int32, (t, LANES), 1)
    gcol = jnp.where(lane_c < HEADS, gates, _log_sigmoid(gates))
    grow = gcol.T[0:SUBLANES, :]
    rows3 = jnp.concatenate(list(_split3(grow)) + [jnp.zeros((SUBLANES, t), f32)], axis=0).astype(bf16)
    pool_in = proj(4)
    pbuf_ref[POOL_HALO:POOL_HALO + t, :] = pool_in
    cx = proj(8) * proj(9)
    sbuf_ref[SCONV_HALO:SCONV_HALO + t, :] = cx
    sc_b = proj(7)
    k = proj(1) * (dh ** -0.5)
    q = proj(0)

    conv_copies()
    cacc = jnp.concatenate([conv_block(r0) for r0 in range(0, t, CONV_ROWS)], axis=0)
    ubuf_ref[0:CONV_HALO, :] = ubuf_ref[t:t + CONV_HALO, :]
    mu = jnp.mean(cacc, axis=-1, keepdims=True)
    cen = cacc - mu
    var = jnp.mean(cen * cen, axis=-1, keepdims=True)
    ln = cen * lax.rsqrt(var + EPS) * lng_ref[...] + lnb_ref[...]
    sw = ln * _sigmoid(ln)
    v = proj(2)

    ext = pbuf_ref[...]
    win = []
    acc = ext
    shift = 1
    while shift < max(POOL_WINDOWS):
        acc = acc + pltpu.roll(acc, shift, axis=0)
        shift *= 2
        win.append(acc[POOL_HALO:POOL_HALO + t, :])
    gc = g // len(POOL_WINDOWS)
    grp = lax.broadcasted_iota(jnp.int32, (t, g), 1) // gc
    wsum = win[-1]
    wlen = jnp.full((1, g), float(POOL_WINDOWS[-1]), f32)
    for gi in range(len(POOL_WINDOWS) - 2, -1, -1):
        wsum = jnp.where(grp == gi, win[gi], wsum)
        wlen = jnp.where(grp[0:1, :] == gi, float(POOL_WINDOWS[gi]), wlen)
    pooled = wsum * (1.0 / wlen) - pool_in
    edge = max(POOL_WINDOWS)
    pos = (lax.broadcasted_iota(jnp.int32, (edge, g), 0) + 1).astype(f32)
    first = wsum[0:edge, :] / jnp.minimum(pos, wlen) - pool_in[0:edge, :]
    pooled = jnp.concatenate([jnp.where(j == 0, first, pooled[0:edge, :]), pooled[edge:, :]], axis=0)
    pbuf_ref[0:POOL_HALO, :] = pbuf_ref[t:t + POOL_HALO, :]
    o_pre = proj(3)

    sacc = jnp.zeros((t, g), f32)
    soff = SCONV_HALO - (SCONV_K - 1)
    for tap in range(SCONV_K):
        sacc = sacc + scw_ref[tap:tap + 1, :] * sbuf_ref[soff + tap:soff + tap + t, :]
    sbuf_ref[0:SCONV_HALO, :] = sbuf_ref[t:t + SCONV_HALO, :]
    y_d = sc_b * sacc

    qb = q.astype(bf16)
    vb = v.astype(bf16)
    kt = k.T
    ktb = kt.astype(bf16)
    head_g = lax.broadcasted_iota(jnp.int32, (1, g), 1) // dh
    scores = []
    for h in range(HEADS):
        parts = [jnp.zeros((h * dh, t), bf16), ktb[h * dh:(h + 1) * dh, :], jnp.zeros((g - (h + 1) * dh, t), bf16)]
        scores.append(_dot(qb, jnp.concatenate([p for p in parts if p.shape[0]], axis=0)))
    state = state_ref[...]
    qs = _dot(qb, state.astype(bf16))
    rsum = _dot(rows3, trit_ref[...])
    bcum_row = rsum[0:SUBLANES] + rsum[SUBLANES:2 * SUBLANES] + rsum[2 * SUBLANES:3 * SUBLANES]
    a_row = grow - pltpu.roll(bcum_row, HEADS, axis=0)
    bcum_col = jnp.concatenate([bcum_row, jnp.zeros((LANES - SUBLANES, t), f32)], axis=0).T
    bcol = pltpu.roll(bcum_col, LANES - HEADS, axis=1)

    act = jnp.maximum(_dot(h1b, wup_ref[...]), 0.0)
    actb = (act * act).astype(bf16)

    row_i = lax.broadcasted_iota(jnp.int32, (t, t), 0)
    col_i = lax.broadcasted_iota(jnp.int32, (t, t), 1)
    causal = col_i <= row_i
    m_prev = mlane_ref[...]
    num = jnp.zeros((t, g), f32)
    den_col = jnp.zeros((t, LANES), f32)
    mcol = jnp.zeros((t, LANES), f32)
    wgt_rows = []
    for h in range(HEADS):
        a_h = a_row[h:h + 1, :]
        masked = jnp.where(causal, a_h, -jnp.inf)
        m_h = jnp.maximum(jnp.max(masked, axis=1, keepdims=True), m_prev[:, h:h + 1])
        p_h = scores[h] * jnp.exp(masked - m_h)
        den_col = jnp.where(lane_c == h, jnp.sum(p_h, axis=1, keepdims=True), den_col)
        mcol = jnp.where(lane_c == h, m_h, mcol)
        num = num + _dot(p_h.astype(bf16), jnp.where(head_g == h, vb, jnp.zeros_like(vb)))
        wgt_rows.append(jnp.exp(a_h - m_h[t - 1:t, :]))

    wexp = jnp.concatenate([jnp.broadcast_to(w, (dh, t)) for w in wgt_rows], axis=0)
    kwb = (kt * wexp).astype(bf16)
    v_aug = jnp.concatenate([vb, jnp.ones((t, LANES), bf16)], axis=1)
    upd = _dot(kwb, v_aug)
    m_last = mcol[t - 1:t, :]
    decay_l = jnp.exp(m_prev - m_last)
    decay = jnp.concatenate([_expand_heads(decay_l, dh, g), decay_l], axis=1)
    r_head = lax.broadcasted_iota(jnp.int32, (g, g + LANES), 0) // dh
    c_idx = lax.broadcasted_iota(jnp.int32, (g, g + LANES), 1)
    c_head = jnp.where(c_idx < g, c_idx // dh, c_idx - g)
    state_ref[...] = state * decay + jnp.where(r_head == c_head, upd, 0.0)
    mlane_ref[...] = bcol[t - 1:t, :] + m_last
    y_b = _dot(pooled.astype(bf16), poolw_ref[...]) * pscale_ref[...]
    y_c = _dot(sw.astype(bf16), pww_ref[...]) + pwb_ref[...]

    y = x1 + _dot(actb, wdown_ref[...])
    if final_norm:
        y = y * lax.rsqrt(jnp.mean(y * y, axis=-1, keepdims=True) + EPS) * fg_ref[...]
    o_ref[...] = y

    inter = jnp.exp(m_prev - mcol)
    den = den_col + inter * qs[:, g:]
    dmax = jnp.maximum(jnp.abs(den), jnp.exp(-(bcol + mcol)))
    num = num + _expand_heads(inter, dh, g) * qs[:, :g]
    sq = num * num
    ss = jnp.zeros((t, LANES), f32)
    head_tg = lax.broadcasted_iota(jnp.int32, (t, g), 1) // dh
    for h in range(HEADS):
        ss = jnp.where(lane_c == h, jnp.sum(jnp.where(head_tg == h, sq, 0.0), axis=1, keepdims=True), ss)
    inv_d = 1.0 / dmax
    scale = inv_d * lax.rsqrt((ss * (1.0 / dh)) * inv_d * inv_d + EPS)
    y_a = _sigmoid(o_pre) * (num * _expand_heads(scale, dh, g) * mgain_ref[...])

    mixed = jnp.concatenate([y_a.astype(bf16), y_b.astype(bf16), y_c.astype(bf16), y_d.astype(bf16)], axis=1)
    hand_ref[...] = x + _dot(mixed, wout_ref[...])


def _shared_spec(a):
    return pl.BlockSpec(a.shape, lambda s: (0,) * a.ndim)


def _layer_spec(a, layer):
    index_map = lambda s: (layer,) + (0,) * (a.ndim - 1)
    if a[0].size * a.dtype.itemsize >= BIG_OPERAND_BYTES:
        return pl.BlockSpec((None,) + a.shape[1:], index_map, pipeline_mode=pl.Buffered(1))
    return pl.BlockSpec((None,) + a.shape[1:], index_map)


def _layer_call(x2, seq_len, layer, shared, stacked, final_norm):
    n, d = x2.shape
    g = stacked[5].shape[-1]
    t = shared[0].shape[0]
    tiles = n // t
    return pl.pallas_call(
        functools.partial(_layer_kernel, tiles_per_seq=seq_len // t, final_norm=final_norm),
        grid=(tiles + 1,),
        in_specs=[pl.BlockSpec((t, d), lambda s: (jnp.minimum(s, tiles - 1), 0))]
        + [_shared_spec(a) for a in shared] + [_layer_spec(a, layer) for a in stacked],
        out_specs=pl.BlockSpec((t, d), lambda s: (jnp.maximum(s - 1, 0), 0)),
        out_shape=jax.ShapeDtypeStruct(x2.shape, x2.dtype),
        scratch_shapes=[
            pltpu.VMEM((t, d), f32),
            pltpu.VMEM((g, g + LANES), f32),
            pltpu.VMEM((1, LANES), f32),
            pltpu.VMEM((CONV_HALO + t, g), f32),
            pltpu.VMEM((SUBLANES - 1, CONV_HALO + t, g), f32),
            pltpu.VMEM((POOL_HALO + t, g), f32),
            pltpu.VMEM((SCONV_HALO + t, g), f32),
        ],
        compiler_params=pltpu.CompilerParams(
            dimension_semantics=("arbitrary",), vmem_limit_bytes=VMEM_LIMIT_BYTES),
        name="layer_final" if final_norm else "layer",
    )(x2, *shared, *stacked)


def kernel(x, attn_norm_gain, w_in, mlstm_igate_bias, mlstm_fgate_bias, mlstm_norm_gain, pool_w, pool_scale, conv_w, conv_b, conv_ln_gain, conv_ln_bias, conv_pw_w, conv_pw_b, sconv_w, w_out, mlp_norm_gain, w_up, w_down, final_norm_gain):
    depth = w_in.shape[0]
    b, s, d = x.shape
    g = mlstm_norm_gain.shape[1]
    t = min(SEQ_TILE, s)
    assert s % t == 0 and t % LANES == 0
    gate_lo = 4 * g
    gate_hi = gate_lo + 2 * HEADS
    gate_pad = LANES - 2 * HEADS
    row = lambda a: a[:, None, :]
    pool_groups = pool_w.shape[1]
    pool_bd = (jnp.eye(pool_groups, dtype=pool_w.dtype)[None, :, None, :, None]
               * pool_w[:, :, :, None, :]).reshape(depth, g, g)
    stacked = (
        row(attn_norm_gain),
        w_in[:, :, :gate_lo].astype(bf16),
        w_in[:, :, gate_hi:].astype(bf16),
        jnp.pad(w_in[:, :, gate_lo:gate_hi], ((0, 0), (0, 0), (0, gate_pad))).astype(bf16),
        row(jnp.pad(jnp.concatenate([mlstm_igate_bias, mlstm_fgate_bias], axis=1), ((0, 0), (0, gate_pad)))),
        row(mlstm_norm_gain), pool_bd.astype(bf16), row(pool_scale),
        conv_w, row(conv_b), row(conv_ln_gain), row(conv_ln_bias),
        conv_pw_w.astype(bf16), row(conv_pw_b), sconv_w, w_out.astype(bf16),
        row(mlp_norm_gain), w_up.astype(bf16), w_down.astype(bf16))
    shared = (jnp.triu(jnp.ones((t, t), bf16)),
              final_norm_gain[None, :])
    x2 = x.reshape(b * s, d)
    for l in range(depth):
        x2 = _layer_call(x2, s, l, shared, stacked, final_norm=(l == depth - 1))
    return x2.reshape(b, s, d)
```

```python
import functools

import jax
import jax.numpy as jnp
from jax import lax
from jax.experimental import pallas as pl
from jax.experimental.pallas import tpu as pltpu

EPS = 1e-6
HEADS = 4
POOL_WINDOWS = (2, 4, 8, 16)
assert all(w & (w - 1) == 0 for w in POOL_WINDOWS)
CONV_K = 31
SCONV_K = 3
LANES = 128
SUBLANES = 8
SEQ_TILE = 256
SUBTILES = 2
BIG_OPERAND_BYTES = 1 << 20
CONV_HALO = 32
POOL_HALO = 16
SCONV_HALO = 8
CONV_ROWS = 32
VMEM_LIMIT_BYTES = 56 * 1024 * 1024

f32 = jnp.float32
bf16 = jnp.bfloat16


def _dot(a, b):
    return jnp.dot(a, b, preferred_element_type=f32)


def _sigmoid(x):
    return 1.0 / (1.0 + jnp.exp(-x))


def _log_sigmoid(x):
    return jnp.minimum(x, 0.0) - jnp.log(1.0 + jnp.exp(-jnp.abs(x)))


def _split3(x):
    hi = x.astype(bf16).astype(f32)
    rest = x - hi
    mid = rest.astype(bf16).astype(f32)
    return hi, mid, rest - mid


def _expand_heads(col, group, width):
    t = col.shape[0]
    head = lax.broadcasted_iota(jnp.int32, (t, width), 1) // group
    out = jnp.broadcast_to(col[:, HEADS - 1:HEADS], (t, width))
    for h in range(HEADS - 2, -1, -1):
        out = jnp.where(head == h, jnp.broadcast_to(col[:, h:h + 1], (t, width)), out)
    return out


def _layer_kernel(x_ref, trit_ref, fg_ref, ng_ref, wqkvo_ref, wrest_ref, wgate_ref, gbias_ref, mgain_ref,
                  poolw_ref, pscale_ref, convw_ref, convb_ref, lng_ref, lnb_ref, pww_ref, pwb_ref, scw_ref,
                  wout_ref, mlpg_ref, wup_ref, wdown_ref,
                  o_ref,
                  hand_ref, state_ref, mlane_ref, ubuf_ref, cbuf_ref, pbuf_ref, sbuf_ref,
                  *, tiles_per_seq, final_norm):
    t = x_ref.shape[0] // SUBTILES
    g = mgain_ref.shape[1]
    dh = g // HEADS
    step = pl.program_id(0)
    j = lax.rem(step * SUBTILES, tiles_per_seq)

    @pl.when(step == 0)
    def _():
        hand_ref[...] = jnp.zeros_like(hand_ref)

    @pl.when(j == 0)
    def _():
        state_ref[...] = jnp.zeros_like(state_ref)
        mlane_ref[...] = jnp.zeros_like(mlane_ref)
        ubuf_ref[0:CONV_HALO, :] = jnp.zeros((CONV_HALO, g), f32)
        pbuf_ref[0:POOL_HALO, :] = jnp.zeros((POOL_HALO, g), f32)
        sbuf_ref[0:SCONV_HALO, :] = jnp.zeros((SCONV_HALO, g), f32)

    def run_tile(sub):
        rows = slice(sub * t, (sub + 1) * t)
        j_sub = j + sub
        x = x_ref[rows, :]
        hn = x * lax.rsqrt(jnp.mean(x * x, axis=-1, keepdims=True) + EPS) * ng_ref[...]
        hb = hn.astype(bf16)

        def proj(i):
            if i < 4:
                return _dot(hb, wqkvo_ref[:, i * g:(i + 1) * g])
            return _dot(hb, wrest_ref[:, (i - 4) * g:(i - 3) * g])

        off = CONV_HALO - (CONV_K - 1)
        tap_src = []
        for tap in range(CONV_K):
            b = (off + tap) % SUBLANES
            tap_src.append((ubuf_ref, off + tap) if b == 0 else (cbuf_ref.at[b - 1], off + tap - b))

        def conv_copies():
            for b in range(1, SUBLANES):
                n_rows = max(r for tap, (_, r) in enumerate(tap_src) if (off + tap) % SUBLANES == b) + t
                cbuf_ref[b - 1, 0:n_rows, :] = ubuf_ref[b:b + n_rows, :]

        def conv_block(r0):
            acc = jnp.broadcast_to(convb_ref[...], (CONV_ROWS, g))
            for tap, (src, r) in enumerate(tap_src):
                acc = acc + convw_ref[tap:tap + 1, :] * src[r + r0:r + r0 + CONV_ROWS, :]
            return acc

        gates = _dot(hb, wgate_ref[...]) + gbias_ref[...]

        x1 = hand_ref[rows, :]
        h1b = (x1 * lax.rsqrt(jnp.mean(x1 * x1, axis=-1, keepdims=True) + EPS) * mlpg_ref[...]).astype(bf16)

        glu = proj(5) * _sigmoid(proj(6))
        ubuf_ref[CONV_HALO:CONV_HALO + t, :] = glu
        lane_c = lax.broadcasted_iota(jnp.int32, (t, LANES), 1)
        gcol = jnp.where(lane_c < HEADS, gates, _log_sigmoid(gates))
        grow = gcol.T[0:SUBLANES, :]
        rows3 = jnp.concatenate(list(_split3(grow)) + [jnp.zeros((SUBLANES, t), f32)], axis=0).astype(bf16)
        pool_in = proj(4)
        pbuf_ref[POOL_HALO:POOL_HALO + t, :] = pool_in
        cx = proj(8) * proj(9)
        sbuf_ref[SCONV_HALO:SCONV_HALO + t, :] = cx
        sc_b = proj(7)
        k = proj(1) * (dh ** -0.5)
        q = proj(0)

        conv_copies()
        cacc = jnp.concatenate([conv_block(r0) for r0 in range(0, t, CONV_ROWS)], axis=0)
        ubuf_ref[0:CONV_HALO, :] = ubuf_ref[t:t + CONV_HALO, :]
        mu = jnp.mean(cacc, axis=-1, keepdims=True)
        cen = cacc - mu
        var = jnp.mean(cen * cen, axis=-1, keepdims=True)
        ln = cen * lax.rsqrt(var + EPS) * lng_ref[...] + lnb_ref[...]
        sw = ln * _sigmoid(ln)
        v = proj(2)

        ext = pbuf_ref[...]
        win = []
        acc = ext
        shift = 1
        while shift < max(POOL_WINDOWS):
            acc = acc + pltpu.roll(acc, shift, axis=0)
            shift *= 2
            win.append(acc[POOL_HALO:POOL_HALO + t, :])
        gc = g // len(POOL_WINDOWS)
        grp = lax.broadcasted_iota(jnp.int32, (t, g), 1) // gc
        wsum = win[-1]
        wlen = jnp.full((1, g), float(POOL_WINDOWS[-1]), f32)
        for gi in range(len(POOL_WINDOWS) - 2, -1, -1):
            wsum = jnp.where(grp == gi, win[gi], wsum)
            wlen = jnp.where(grp[0:1, :] == gi, float(POOL_WINDOWS[gi]), wlen)
        pooled = wsum * (1.0 / wlen) - pool_in
        edge = max(POOL_WINDOWS)
        pos = (lax.broadcasted_iota(jnp.int32, (edge, g), 0) + 1).astype(f32)
        first = wsum[0:edge, :] / jnp.minimum(pos, wlen) - pool_in[0:edge, :]
        pooled = jnp.concatenate([jnp.where(j_sub == 0, first, pooled[0:edge, :]), pooled[edge:, :]], axis=0)
        pbuf_ref[0:POOL_HALO, :] = pbuf_ref[t:t + POOL_HALO, :]
        o_pre = proj(3)

        sacc = jnp.zeros((t, g), f32)
        soff = SCONV_HALO - (SCONV_K - 1)
        for tap in range(SCONV_K):
            sacc = sacc + scw_ref[tap:tap + 1, :] * sbuf_ref[soff + tap:soff + tap + t, :]
        sbuf_ref[0:SCONV_HALO, :] = sbuf_ref[t:t + SCONV_HALO, :]
        y_d = sc_b * sacc

        qb = q.astype(bf16)
        vb = v.astype(bf16)
        kt = k.T
        ktb = kt.astype(bf16)
        head_g = lax.broadcasted_iota(jnp.int32, (1, g), 1) // dh
        scores = []
        for h in range(HEADS):
            parts = [jnp.zeros((h * dh, t), bf16), ktb[h * dh:(h + 1) * dh, :], jnp.zeros((g - (h + 1) * dh, t), bf16)]
            scores.append(_dot(qb, jnp.concatenate([p for p in parts if p.shape[0]], axis=0)))
        state = state_ref[...]
        qs = _dot(qb, state.astype(bf16))
        rsum = _dot(rows3, trit_ref[...])
        bcum_row = rsum[0:SUBLANES] + rsum[SUBLANES:2 * SUBLANES] + rsum[2 * SUBLANES:3 * SUBLANES]
        a_row = grow - pltpu.roll(bcum_row, HEADS, axis=0)
        bcum_col = jnp.concatenate([bcum_row, jnp.zeros((LANES - SUBLANES, t), f32)], axis=0).T
        bcol = pltpu.roll(bcum_col, LANES - HEADS, axis=1)

        act = jnp.maximum(_dot(h1b, wup_ref[...]), 0.0)
        actb = (act * act).astype(bf16)

        row_i = lax.broadcasted_iota(jnp.int32, (t, t), 0)
        col_i = lax.broadcasted_iota(jnp.int32, (t, t), 1)
        causal = col_i <= row_i
        m_prev = mlane_ref[...]
        num = jnp.zeros((t, g), f32)
        den_col = jnp.zeros((t, LANES), f32)
        mcol = jnp.zeros((t, LANES), f32)
        wgt_rows = []
        for h in range(HEADS):
            a_h = a_row[h:h + 1, :]
            masked = jnp.where(causal, a_h, -jnp.inf)
            m_h = jnp.maximum(jnp.max(masked, axis=1, keepdims=True), m_prev[:, h:h + 1])
            p_h = scores[h] * jnp.exp(masked - m_h)
            den_col = jnp.where(lane_c == h, jnp.sum(p_h, axis=1, keepdims=True), den_col)
            mcol = jnp.where(lane_c == h, m_h, mcol)
            num = num + _dot(p_h.astype(bf16), jnp.where(head_g == h, vb, jnp.zeros_like(vb)))
            wgt_rows.append(jnp.exp(a_h - m_h[t - 1:t, :]))

        wexp = jnp.concatenate([jnp.broadcast_to(w, (dh, t)) for w in wgt_rows], axis=0)
        kwb = (kt * wexp).astype(bf16)
        v_aug = jnp.concatenate([vb, jnp.ones((t, LANES), bf16)], axis=1)
        upd = _dot(kwb, v_aug)
        m_last = mcol[t - 1:t, :]
        decay_l = jnp.exp(m_prev - m_last)
        decay = jnp.concatenate([_expand_heads(decay_l, dh, g), decay_l], axis=1)
        r_head = lax.broadcasted_iota(jnp.int32, (g, g + LANES), 0) // dh
        c_idx = lax.broadcasted_iota(jnp.int32, (g, g + LANES), 1)
        c_head = jnp.where(c_idx < g, c_idx // dh, c_idx - g)
        state_ref[...] = state * decay + jnp.where(r_head == c_head, upd, 0.0)
        mlane_ref[...] = bcol[t - 1:t, :] + m_last
        y_b = _dot(pooled.astype(bf16), poolw_ref[...]) * pscale_ref[...]
        y_c = _dot(sw.astype(bf16), pww_ref[...]) + pwb_ref[...]

        y = x1 + _dot(actb, wdown_ref[...])
        if final_norm:
            y = y * lax.rsqrt(jnp.mean(y * y, axis=-1, keepdims=True) + EPS) * fg_ref[...]
        o_ref[rows, :] = y

        inter = jnp.exp(m_prev - mcol)
        den = den_col + inter * qs[:, g:]
        dmax = jnp.maximum(jnp.abs(den), jnp.exp(-(bcol + mcol)))
        num = num + _expand_heads(inter, dh, g) * qs[:, :g]
        sq = num * num
        ss = jnp.zeros((t, LANES), f32)
        head_tg = lax.broadcasted_iota(jnp.int32, (t, g), 1) // dh
        for h in range(HEADS):
            ss = jnp.where(lane_c == h, jnp.sum(jnp.where(head_tg == h, sq, 0.0), axis=1, keepdims=True), ss)
        inv_d = 1.0 / dmax
        scale = inv_d * lax.rsqrt((ss * (1.0 / dh)) * inv_d * inv_d + EPS)
        y_a = _sigmoid(o_pre) * (num * _expand_heads(scale, dh, g) * mgain_ref[...])

        mixed = jnp.concatenate([y_a.astype(bf16), y_b.astype(bf16), y_c.astype(bf16), y_d.astype(bf16)], axis=1)
        yield
        hand_ref[rows, :] = x + _dot(mixed, wout_ref[...])

    tiles = [run_tile(sub) for sub in range(SUBTILES)]
    for tile in tiles:
        next(tile)
    for tile in tiles:
        next(tile, None)


def _shared_spec(a):
    return pl.BlockSpec(a.shape, lambda s: (0,) * a.ndim)


def _layer_spec(a, layer):
    index_map = lambda s: (layer,) + (0,) * (a.ndim - 1)
    if a[0].size * a.dtype.itemsize >= BIG_OPERAND_BYTES:
        return pl.BlockSpec((None,) + a.shape[1:], index_map, pipeline_mode=pl.Buffered(1))
    return pl.BlockSpec((None,) + a.shape[1:], index_map)


def _layer_call(x2, seq_len, layer, shared, stacked, final_norm):
    n, d = x2.shape
    g = stacked[5].shape[-1]
    t = shared[0].shape[0]
    rows = SUBTILES * t
    assert (seq_len // t) % SUBTILES == 0 and n % rows == 0
    blocks = n // rows
    return pl.pallas_call(
        functools.partial(_layer_kernel, tiles_per_seq=seq_len // t, final_norm=final_norm),
        grid=(blocks + 1,),
        in_specs=[pl.BlockSpec((rows, d), lambda s: (jnp.minimum(s, blocks - 1), 0))]
        + [_shared_spec(a) for a in shared] + [_layer_spec(a, layer) for a in stacked],
        out_specs=pl.BlockSpec((rows, d), lambda s: (jnp.maximum(s - 1, 0), 0)),
        out_shape=jax.ShapeDtypeStruct(x2.shape, x2.dtype),
        scratch_shapes=[
            pltpu.VMEM((rows, d), f32),
            pltpu.VMEM((g, g + LANES), f32),
            pltpu.VMEM((1, LANES), f32),
            pltpu.VMEM((CONV_HALO + t, g), f32),
            pltpu.VMEM((SUBLANES - 1, CONV_HALO + t, g), f32),
            pltpu.VMEM((POOL_HALO + t, g), f32),
            pltpu.VMEM((SCONV_HALO + t, g), f32),
        ],
        compiler_params=pltpu.CompilerParams(
            dimension_semantics=("arbitrary",), vmem_limit_bytes=VMEM_LIMIT_BYTES),
        name="layer_final" if final_norm else "layer",
    )(x2, *shared, *stacked)


def kernel(x, attn_norm_gain, w_in, mlstm_igate_bias, mlstm_fgate_bias, mlstm_norm_gain, pool_w, pool_scale, conv_w, conv_b, conv_ln_gain, conv_ln_bias, conv_pw_w, conv_pw_b, sconv_w, w_out, mlp_norm_gain, w_up, w_down, final_norm_gain):
    depth = w_in.shape[0]
    b, s, d = x.shape
    g = mlstm_norm_gain.shape[1]
    t = min(SEQ_TILE, s)
    assert s % t == 0 and t % LANES == 0
    gate_lo = 4 * g
    gate_hi = gate_lo + 2 * HEADS
    gate_pad = LANES - 2 * HEADS
    row = lambda a: a[:, None, :]
    pool_groups = pool_w.shape[1]
    pool_bd = (jnp.eye(pool_groups, dtype=pool_w.dtype)[None, :, None, :, None]
               * pool_w[:, :, :, None, :]).reshape(depth, g, g)
    stacked = (
        row(attn_norm_gain),
        w_in[:, :, :gate_lo].astype(bf16),
        w_in[:, :, gate_hi:].astype(bf16),
        jnp.pad(w_in[:, :, gate_lo:gate_hi], ((0, 0), (0, 0), (0, gate_pad))).astype(bf16),
        row(jnp.pad(jnp.concatenate([mlstm_igate_bias, mlstm_fgate_bias], axis=1), ((0, 0), (0, gate_pad)))),
        row(mlstm_norm_gain), pool_bd.astype(bf16), row(pool_scale),
        conv_w, row(conv_b), row(conv_ln_gain), row(conv_ln_bias),
        conv_pw_w.astype(bf16), row(conv_pw_b), sconv_w, w_out.astype(bf16),
        row(mlp_norm_gain), w_up.astype(bf16), w_down.astype(bf16))
    shared = (jnp.triu(jnp.ones((t, t), bf16)),
              final_norm_gain[None, :])
    x2 = x.reshape(b * s, d)
    for l in range(depth):
        x2 = _layer_call(x2, s, l, shared, stacked, final_norm=(l == depth - 1))
    return x2.reshape(b, s, d)
```

```python
import functools

import jax
import jax.numpy as jnp
from jax import lax
from jax.experimental import pallas as pl
from jax.experimental.pallas import tpu as pltpu

EPS = 1e-6
HEADS = 4
POOL_WINDOWS = (2, 4, 8, 16)
assert all(w & (w - 1) == 0 for w in POOL_WINDOWS)
CONV_K = 31
SCONV_K = 3
LANES = 128
SUBLANES = 8
SEQ_TILE = 256
SUBTILES = 2
BIG_OPERAND_BYTES = 1 << 20
CONV_HALO = 32
POOL_HALO = 16
SCONV_HALO = 8
CONV_ROWS = 32
VMEM_LIMIT_BYTES = 56 * 1024 * 1024

f32 = jnp.float32
bf16 = jnp.bfloat16


def _dot(a, b):
    return jnp.dot(a, b, preferred_element_type=f32)


def _sigmoid(x):
    return 1.0 / (1.0 + jnp.exp(-x))


def _log_sigmoid(x):
    return jnp.minimum(x, 0.0) - jnp.log(1.0 + jnp.exp(-jnp.abs(x)))


def _split3(x):
    hi = x.astype(bf16).astype(f32)
    rest = x - hi
    mid = rest.astype(bf16).astype(f32)
    return hi, mid, rest - mid


def _expand_heads(col, group, width):
    t = col.shape[0]
    head = lax.broadcasted_iota(jnp.int32, (t, width), 1) // group
    out = jnp.broadcast_to(col[:, HEADS - 1:HEADS], (t, width))
    for h in range(HEADS - 2, -1, -1):
        out = jnp.where(head == h, jnp.broadcast_to(col[:, h:h + 1], (t, width)), out)
    return out


def _layer_kernel(x_ref, trit_ref, fg_ref, ng_ref, wqkvo_ref, wrest_ref, wgate_ref, gbias_ref, mgain_ref,
                  poolw_ref, pscale_ref, convw_ref, convb_ref, lng_ref, lnb_ref, pww_ref, pwb_ref, scw_ref,
                  wout_ref, mlpg_ref, wup_ref, wdown_ref,
                  o_ref,
                  hand_ref, state_ref, mlane_ref, ubuf_ref, cbuf_ref, pbuf_ref, sbuf_ref,
                  *, tiles_per_seq, final_norm):
    t = x_ref.shape[0] // SUBTILES
    g = mgain_ref.shape[1]
    dh = g // HEADS
    step = pl.program_id(0)
    j = lax.rem(step * SUBTILES, tiles_per_seq)

    @pl.when(step == 0)
    def _():
        hand_ref[...] = jnp.zeros_like(hand_ref)

    @pl.when(j == 0)
    def _():
        state_ref[...] = jnp.zeros_like(state_ref)
        mlane_ref[...] = jnp.zeros_like(mlane_ref)
        ubuf_ref[0:CONV_HALO, :] = jnp.zeros((CONV_HALO, g), f32)
        pbuf_ref[0:POOL_HALO, :] = jnp.zeros((POOL_HALO, g), f32)
        sbuf_ref[0:SCONV_HALO, :] = jnp.zeros((SCONV_HALO, g), f32)

    def run_tile(sub):
        rows = slice(sub * t, (sub + 1) * t)
        j_sub = j + sub
        x = x_ref[rows, :]
        hn = x * lax.rsqrt(jnp.mean(x * x, axis=-1, keepdims=True) + EPS) * ng_ref[...]
        hb = hn.astype(bf16)

        def proj(i):
            if i < 4:
                return _dot(hb, wqkvo_ref[:, i * g:(i + 1) * g])
            return _dot(hb, wrest_ref[:, (i - 4) * g:(i - 3) * g])

        off = CONV_HALO - (CONV_K - 1)
        tap_src = []
        for tap in range(CONV_K):
            b = (off + tap) % SUBLANES
            tap_src.append((ubuf_ref, off + tap) if b == 0 else (cbuf_ref.at[b - 1], off + tap - b))

        def conv_copies():
            for b in range(1, SUBLANES):
                n_rows = max(r for tap, (_, r) in enumerate(tap_src) if (off + tap) % SUBLANES == b) + t
                cbuf_ref[b - 1, 0:n_rows, :] = ubuf_ref[b:b + n_rows, :]

        def conv_block(r0):
            acc = jnp.broadcast_to(convb_ref[...], (CONV_ROWS, g))
            for tap, (src, r) in enumerate(tap_src):
                acc = acc + convw_ref[tap:tap + 1, :] * src[r + r0:r + r0 + CONV_ROWS, :]
            return acc

        gates = _dot(hb, wgate_ref[...]) + gbias_ref[...]

        x1 = hand_ref[rows, :]
        h1b = (x1 * lax.rsqrt(jnp.mean(x1 * x1, axis=-1, keepdims=True) + EPS) * mlpg_ref[...]).astype(bf16)

        glu = proj(5) * _sigmoid(proj(6))
        ubuf_ref[CONV_HALO:CONV_HALO + t, :] = glu
        lane_c = lax.broadcasted_iota(jnp.int32, (t, LANES), 1)
        gcol = jnp.where(lane_c < HEADS, gates, _log_sigmoid(gates))
        grow = gcol.T[0:SUBLANES, :]
        rows3 = jnp.concatenate(list(_split3(grow)) + [jnp.zeros((SUBLANES, t), f32)], axis=0).astype(bf16)
        pool_in = proj(4)
        pbuf_ref[POOL_HALO:POOL_HALO + t, :] = pool_in
        cx = proj(8) * proj(9)
        sbuf_ref[SCONV_HALO:SCONV_HALO + t, :] = cx
        sc_b = proj(7)
        k = proj(1) * (dh ** -0.5)
        q = proj(0)

        conv_copies()
        cacc = jnp.concatenate([conv_block(r0) for r0 in range(0, t, CONV_ROWS)], axis=0)
        ubuf_ref[0:CONV_HALO, :] = ubuf_ref[t:t + CONV_HALO, :]
        mu = jnp.mean(cacc, axis=-1, keepdims=True)
        cen = cacc - mu
        var = jnp.mean(cen * cen, axis=-1, keepdims=True)
        ln = cen * lax.rsqrt(var + EPS) * lng_ref[...] + lnb_ref[...]
        sw = ln * _sigmoid(ln)
        v = proj(2)

        ext = pbuf_ref[...]
        win = []
        acc = ext
        shift = 1
        while shift < max(POOL_WINDOWS):
            acc = acc + pltpu.roll(acc, shift, axis=0)
            shift *= 2
            win.append(acc[POOL_HALO:POOL_HALO + t, :])
        gc = g // len(POOL_WINDOWS)
        grp = lax.broadcasted_iota(jnp.int32, (t, g), 1) // gc
        wsum = win[-1]
        wlen = jnp.full((1, g), float(POOL_WINDOWS[-1]), f32)
        for gi in range(len(POOL_WINDOWS) - 2, -1, -1):
            wsum = jnp.where(grp == gi, win[gi], wsum)
            wlen = jnp.where(grp[0:1, :] == gi, float(POOL_WINDOWS[gi]), wlen)
        pooled = wsum * (1.0 / wlen) - pool_in
        edge = max(POOL_WINDOWS)
        pos = (lax.broadcasted_iota(jnp.int32, (edge, g), 0) + 1).astype(f32)
        first = wsum[0:edge, :] / jnp.minimum(pos, wlen) - pool_in[0:edge, :]
        pooled = jnp.concatenate([jnp.where(j_sub == 0, first, pooled[0:edge, :]), pooled[edge:, :]], axis=0)
        pbuf_ref[0:POOL_HALO, :] = pbuf_ref[t:t + POOL_HALO, :]
        o_pre = proj(3)

        sacc = jnp.zeros((t, g), f32)
        soff = SCONV_HALO - (SCONV_K - 1)
        for tap in range(SCONV_K):
            sacc = sacc + scw_ref[tap:tap + 1, :] * sbuf_ref[soff + tap:soff + tap + t, :]
        sbuf_ref[0:SCONV_HALO, :] = sbuf_ref[t:t + SCONV_HALO, :]
        y_d = sc_b * sacc

        qb = q.astype(bf16)
        vb = v.astype(bf16)
        kt = k.T
        ktb = kt.astype(bf16)
        head_g = lax.broadcasted_iota(jnp.int32, (1, g), 1) // dh
        scores = []
        for h in range(HEADS):
            parts = [jnp.zeros((h * dh, t), bf16), ktb[h * dh:(h + 1) * dh, :], jnp.zeros((g - (h + 1) * dh, t), bf16)]
            scores.append(_dot(qb, jnp.concatenate([p for p in parts if p.shape[0]], axis=0)))
        state = state_ref[...]
        qs = _dot(qb, state.astype(bf16))
        rsum = _dot(rows3, trit_ref[...])
        bcum_row = rsum[0:SUBLANES] + rsum[SUBLANES:2 * SUBLANES] + rsum[2 * SUBLANES:3 * SUBLANES]
        a_row = grow - pltpu.roll(bcum_row, HEADS, axis=0)
        bcum_col = jnp.concatenate([bcum_row, jnp.zeros((LANES - SUBLANES, t), f32)], axis=0).T
        bcol = pltpu.roll(bcum_col, LANES - HEADS, axis=1)

        act = jnp.maximum(_dot(h1b, wup_ref[...]), 0.0)
        actb = (act * act).astype(bf16)

        row_i = lax.broadcasted_iota(jnp.int32, (t, t), 0)
        col_i = lax.broadcasted_iota(jnp.int32, (t, t), 1)
        causal = col_i <= row_i
        m_prev = mlane_ref[...]
        num = jnp.zeros((t, g), f32)
        den_col = jnp.zeros((t, LANES), f32)
        mcol = jnp.zeros((t, LANES), f32)
        wgt_rows = []
        for h in range(HEADS):
            a_h = a_row[h:h + 1, :]
            masked = jnp.where(causal, a_h, -jnp.inf)
            m_h = jnp.maximum(jnp.max(masked, axis=1, keepdims=True), m_prev[:, h:h + 1])
            p_h = scores[h] * jnp.exp(masked - m_h)
            den_col = jnp.where(lane_c == h, jnp.sum(p_h, axis=1, keepdims=True), den_col)
            mcol = jnp.where(lane_c == h, m_h, mcol)
            num = num + _dot(p_h.astype(bf16), jnp.where(head_g == h, vb, jnp.zeros_like(vb)))
            wgt_rows.append(jnp.exp(a_h - m_h[t - 1:t, :]))

        wexp = jnp.concatenate([jnp.broadcast_to(w, (dh, t)) for w in wgt_rows], axis=0)
        kwb = (kt * wexp).astype(bf16)
        v_aug = jnp.concatenate([vb, jnp.ones((t, LANES), bf16)], axis=1)
        upd = _dot(kwb, v_aug)
        m_last = mcol[t - 1:t, :]
        decay_l = jnp.exp(m_prev - m_last)
        decay = jnp.concatenate([_expand_heads(decay_l, dh, g), decay_l], axis=1)
        r_head = lax.broadcasted_iota(jnp.int32, (g, g + LANES), 0) // dh
        c_idx = lax.broadcasted_iota(jnp.int32, (g, g + LANES), 1)
        c_head = jnp.where(c_idx < g, c_idx // dh, c_idx - g)
        state_ref[...] = state * decay + jnp.where(r_head == c_head, upd, 0.0)
        mlane_ref[...] = bcol[t - 1:t, :] + m_last
        y_b = _dot(pooled.astype(bf16), poolw_ref[...]) * pscale_ref[...]
        y_c = _dot(sw.astype(bf16), pww_ref[...]) + pwb_ref[...]

        y = x1 + _dot(actb, wdown_ref[...])
        if final_norm:
            y = y * lax.rsqrt(jnp.mean(y * y, axis=-1, keepdims=True) + EPS) * fg_ref[...]
        o_ref[rows, :] = y

        inter = jnp.exp(m_prev - mcol)
        den = den_col + inter * qs[:, g:]
        dmax = jnp.maximum(jnp.abs(den), jnp.exp(-(bcol + mcol)))
        num = num + _expand_heads(inter, dh, g) * qs[:, :g]
        sq = num * num
        ss = jnp.zeros((t, LANES), f32)
        head_tg = lax.broadcasted_iota(jnp.int32, (t, g), 1) // dh
        for h in range(HEADS):
            ss = jnp.where(lane_c == h, jnp.sum(jnp.where(head_tg == h, sq, 0.0), axis=1, keepdims=True), ss)
        inv_d = 1.0 / dmax
        scale = inv_d * lax.rsqrt((ss * (1.0 / dh)) * inv_d * inv_d + EPS)
        y_a = _sigmoid(o_pre) * (num * _expand_heads(scale, dh, g) * mgain_ref[...])

        mixed = jnp.concatenate([y_a.astype(bf16), y_b.astype(bf16), y_c.astype(bf16), y_d.astype(bf16)], axis=1)
        yield
        hand_ref[rows, :] = x + _dot(mixed, wout_ref[...])

    tiles = [run_tile(sub) for sub in range(SUBTILES)]
    for tile in tiles:
        next(tile)
    for tile in tiles:
        next(tile, None)


def _shared_spec(a):
    return pl.BlockSpec(a.shape, lambda s: (0,) * a.ndim)


def _layer_spec(a, layer):
    a, width, k = a if isinstance(a, tuple) else (a, a.shape[-1], 0)
    block = (None,) + a.shape[1:-1] + (width,)
    index_map = lambda s: (layer,) + (0,) * (a.ndim - 2) + (k,)
    if a.size // a.shape[0] // a.shape[-1] * width * a.dtype.itemsize >= BIG_OPERAND_BYTES:
        return pl.BlockSpec(block, index_map, pipeline_mode=pl.Buffered(1))
    return pl.BlockSpec(block, index_map)


def _layer_call(x2, seq_len, layer, shared, stacked, final_norm):
    n, d = x2.shape
    g = stacked[5].shape[-1]
    t = shared[0].shape[0]
    rows = SUBTILES * t
    assert (seq_len // t) % SUBTILES == 0 and n % rows == 0
    blocks = n // rows
    return pl.pallas_call(
        functools.partial(_layer_kernel, tiles_per_seq=seq_len // t, final_norm=final_norm),
        grid=(blocks + 1,),
        in_specs=[pl.BlockSpec((rows, d), lambda s: (jnp.minimum(s, blocks - 1), 0))]
        + [_shared_spec(a) for a in shared] + [_layer_spec(a, layer) for a in stacked],
        out_specs=pl.BlockSpec((rows, d), lambda s: (jnp.maximum(s - 1, 0), 0)),
        out_shape=jax.ShapeDtypeStruct(x2.shape, x2.dtype),
        scratch_shapes=[
            pltpu.VMEM((rows, d), f32),
            pltpu.VMEM((g, g + LANES), f32),
            pltpu.VMEM((1, LANES), f32),
            pltpu.VMEM((CONV_HALO + t, g), f32),
            pltpu.VMEM((SUBLANES - 1, CONV_HALO + t, g), f32),
            pltpu.VMEM((POOL_HALO + t, g), f32),
            pltpu.VMEM((SCONV_HALO + t, g), f32),
        ],
        compiler_params=pltpu.CompilerParams(
            dimension_semantics=("arbitrary",), vmem_limit_bytes=VMEM_LIMIT_BYTES),
        name="layer_final" if final_norm else "layer",
    )(x2, *shared, *[a[0] if isinstance(a, tuple) else a for a in stacked])


def kernel(x, attn_norm_gain, w_in, mlstm_igate_bias, mlstm_fgate_bias, mlstm_norm_gain, pool_w, pool_scale, conv_w, conv_b, conv_ln_gain, conv_ln_bias, conv_pw_w, conv_pw_b, sconv_w, w_out, mlp_norm_gain, w_up, w_down, final_norm_gain):
    depth = w_in.shape[0]
    b, s, d = x.shape
    g = mlstm_norm_gain.shape[1]
    t = min(SEQ_TILE, s)
    assert s % t == 0 and t % LANES == 0
    gate_lo = 4 * g
    gate_hi = gate_lo + 2 * HEADS
    gate_pad = LANES - 2 * HEADS
    assert gate_lo % LANES == 0 and gate_lo + LANES <= w_in.shape[2]
    w_in_b = w_in.astype(bf16)
    row = lambda a: a[:, None, :]
    pool_groups = pool_w.shape[1]
    pool_bd = (jnp.eye(pool_groups, dtype=pool_w.dtype)[None, :, None, :, None]
               * pool_w[:, :, :, None, :]).reshape(depth, g, g)
    stacked = (
        row(attn_norm_gain),
        (w_in_b, gate_lo, 0),
        w_in_b[:, :, gate_hi:],
        (w_in_b, LANES, gate_lo // LANES),
        row(jnp.pad(jnp.concatenate([mlstm_igate_bias, mlstm_fgate_bias], axis=1), ((0, 0), (0, gate_pad)))),
        row(mlstm_norm_gain), pool_bd.astype(bf16), row(pool_scale),
        conv_w, row(conv_b), row(conv_ln_gain), row(conv_ln_bias),
        conv_pw_w.astype(bf16), row(conv_pw_b), sconv_w, w_out.astype(bf16),
        row(mlp_norm_gain), w_up.astype(bf16), w_down.astype(bf16))
    shared = (jnp.triu(jnp.ones((t, t), bf16)),
              final_norm_gain[None, :])
    x2 = x.reshape(b * s, d)
    for l in range(depth):
        x2 = _layer_call(x2, s, l, shared, stacked, final_norm=(l == depth - 1))
    return x2.reshape(b, s, d)
```

```python
import functools

import jax
import jax.numpy as jnp
from jax import lax
from jax.experimental import pallas as pl
from jax.experimental.pallas import tpu as pltpu

EPS = 1e-6
HEADS = 4
POOL_WINDOWS = (2, 4, 8, 16)
assert all(w & (w - 1) == 0 for w in POOL_WINDOWS)
CONV_K = 31
SCONV_K = 3
LANES = 128
SUBLANES = 8
SEQ_TILE = 256
SUBTILES = 2
BIG_OPERAND_BYTES = 1 << 20
CONV_HALO = 32
POOL_HALO = 16
SCONV_HALO = 8
CONV_ROWS = 32
VMEM_LIMIT_BYTES = 56 * 1024 * 1024

f32 = jnp.float32
bf16 = jnp.bfloat16


def _dot(a, b):
    return jnp.dot(a, b, preferred_element_type=f32)


def _sigmoid(x):
    return 1.0 / (1.0 + jnp.exp(-x))


def _log_sigmoid(x):
    return jnp.minimum(x, 0.0) - jnp.log(1.0 + jnp.exp(-jnp.abs(x)))


def _split3(x):
    hi = x.astype(bf16).astype(f32)
    rest = x - hi
    mid = rest.astype(bf16).astype(f32)
    return hi, mid, rest - mid


def _expand_heads(col, group, width):
    t = col.shape[0]
    head = lax.broadcasted_iota(jnp.int32, (t, width), 1) // group
    out = jnp.broadcast_to(col[:, HEADS - 1:HEADS], (t, width))
    for h in range(HEADS - 2, -1, -1):
        out = jnp.where(head == h, jnp.broadcast_to(col[:, h:h + 1], (t, width)), out)
    return out


def _layer_kernel(x_ref, trit_ref, fg_ref, ng_ref, wqkvo_ref, wrest_ref, wgate_ref, gbias_ref, mgain_ref,
                  poolw_ref, pscale_ref, convw_ref, convb_ref, lng_ref, lnb_ref, pww_ref, pwb_ref, scw_ref,
                  wout_ref, mlpg_ref, wup_ref, wdown_ref,
                  o_ref,
                  hand_ref, state_ref, mlane_ref, ubuf_ref, cbuf_ref, pbuf_ref, sbuf_ref,
                  *, tiles_per_seq, final_norm):
    t = x_ref.shape[0] // SUBTILES
    g = mgain_ref.shape[1]
    dh = g // HEADS
    step = pl.program_id(0)
    j = lax.rem(step * SUBTILES, tiles_per_seq)

    @pl.when(step == 0)
    def _():
        hand_ref[...] = jnp.zeros_like(hand_ref)

    @pl.when(j == 0)
    def _():
        state_ref[...] = jnp.zeros_like(state_ref)
        mlane_ref[...] = jnp.zeros_like(mlane_ref)
        ubuf_ref[0:CONV_HALO, :] = jnp.zeros((CONV_HALO, g), f32)
        pbuf_ref[0:POOL_HALO, :] = jnp.zeros((POOL_HALO, g), f32)
        sbuf_ref[0:SCONV_HALO, :] = jnp.zeros((SCONV_HALO, g), f32)

    def run_tile(sub):
        rows = slice(sub * t, (sub + 1) * t)
        j_sub = j + sub
        x = x_ref[rows, :]
        hn = x * lax.rsqrt(jnp.mean(x * x, axis=-1, keepdims=True) + EPS) * ng_ref[...]
        hb = hn.astype(bf16)

        def proj(i):
            if i < 4:
                return _dot(hb, wqkvo_ref[:, i * g:(i + 1) * g])
            return _dot(hb, wrest_ref[:, (i - 4) * g:(i - 3) * g])

        off = CONV_HALO - (CONV_K - 1)
        tap_src = []
        for tap in range(CONV_K):
            b = (off + tap) % SUBLANES
            tap_src.append((ubuf_ref, off + tap) if b == 0 else (cbuf_ref.at[b - 1], off + tap - b))

        def conv_copies():
            for b in range(1, SUBLANES):
                n_rows = max(r for tap, (_, r) in enumerate(tap_src) if (off + tap) % SUBLANES == b) + t
                cbuf_ref[b - 1, 0:n_rows, :] = ubuf_ref[b:b + n_rows, :]

        def conv_block(r0):
            acc = jnp.broadcast_to(convb_ref[...], (CONV_ROWS, g))
            for tap, (src, r) in enumerate(tap_src):
                acc = acc + convw_ref[tap:tap + 1, :] * src[r + r0:r + r0 + CONV_ROWS, :]
            return acc

        gates = _dot(hb, wgate_ref[...]) + gbias_ref[...]

        glu = proj(5) * _sigmoid(proj(6))
        ubuf_ref[CONV_HALO:CONV_HALO + t, :] = glu
        lane_c = lax.broadcasted_iota(jnp.int32, (t, LANES), 1)
        gcol = jnp.where(lane_c < HEADS, gates, _log_sigmoid(gates))
        grow = gcol.T[0:SUBLANES, :]
        rows3 = jnp.concatenate(list(_split3(grow)) + [jnp.zeros((SUBLANES, t), f32)], axis=0).astype(bf16)
        pool_in = proj(4)
        pbuf_ref[POOL_HALO:POOL_HALO + t, :] = pool_in
        cx = proj(8) * proj(9)
        sbuf_ref[SCONV_HALO:SCONV_HALO + t, :] = cx
        sc_b = proj(7)
        k = proj(1) * (dh ** -0.5)
        q = proj(0)

        conv_copies()
        cacc = jnp.concatenate([conv_block(r0) for r0 in range(0, t, CONV_ROWS)], axis=0)
        ubuf_ref[0:CONV_HALO, :] = ubuf_ref[t:t + CONV_HALO, :]
        mu = jnp.mean(cacc, axis=-1, keepdims=True)
        cen = cacc - mu
        var = jnp.mean(cen * cen, axis=-1, keepdims=True)
        ln = cen * lax.rsqrt(var + EPS) * lng_ref[...] + lnb_ref[...]
        sw = ln * _sigmoid(ln)
        v = proj(2)

        ext = pbuf_ref[...]
        win = []
        acc = ext
        shift = 1
        while shift < max(POOL_WINDOWS):
            acc = acc + pltpu.roll(acc, shift, axis=0)
            shift *= 2
            win.append(acc[POOL_HALO:POOL_HALO + t, :])
        gc = g // len(POOL_WINDOWS)
        grp = lax.broadcasted_iota(jnp.int32, (t, g), 1) // gc
        wsum = win[-1]
        wlen = jnp.full((1, g), float(POOL_WINDOWS[-1]), f32)
        for gi in range(len(POOL_WINDOWS) - 2, -1, -1):
            wsum = jnp.where(grp == gi, win[gi], wsum)
            wlen = jnp.where(grp[0:1, :] == gi, float(POOL_WINDOWS[gi]), wlen)
        pooled = wsum * (1.0 / wlen) - pool_in
        edge = max(POOL_WINDOWS)
        pos = (lax.broadcasted_iota(jnp.int32, (edge, g), 0) + 1).astype(f32)
        first = wsum[0:edge, :] / jnp.minimum(pos, wlen) - pool_in[0:edge, :]
        pooled = jnp.concatenate([jnp.where(j_sub == 0, first, pooled[0:edge, :]), pooled[edge:, :]], axis=0)
        pbuf_ref[0:POOL_HALO, :] = pbuf_ref[t:t + POOL_HALO, :]
        o_pre = proj(3)

        sacc = jnp.zeros((t, g), f32)
        soff = SCONV_HALO - (SCONV_K - 1)
        for tap in range(SCONV_K):
            sacc = sacc + scw_ref[tap:tap + 1, :] * sbuf_ref[soff + tap:soff + tap + t, :]
        sbuf_ref[0:SCONV_HALO, :] = sbuf_ref[t:t + SCONV_HALO, :]
        y_d = sc_b * sacc

        qb = q.astype(bf16)
        vb = v.astype(bf16)
        kt = k.T
        ktb = kt.astype(bf16)
        head_g = lax.broadcasted_iota(jnp.int32, (1, g), 1) // dh
        scores = []
        for h in range(HEADS):
            parts = [jnp.zeros((h * dh, t), bf16), ktb[h * dh:(h + 1) * dh, :], jnp.zeros((g - (h + 1) * dh, t), bf16)]
            scores.append(_dot(qb, jnp.concatenate([p for p in parts if p.shape[0]], axis=0)))
        rsum = _dot(rows3, trit_ref[...])
        bcum_row = rsum[0:SUBLANES] + rsum[SUBLANES:2 * SUBLANES] + rsum[2 * SUBLANES:3 * SUBLANES]
        a_row = grow - pltpu.roll(bcum_row, HEADS, axis=0)
        bcum_col = jnp.concatenate([bcum_row, jnp.zeros((LANES - SUBLANES, t), f32)], axis=0).T
        bcol = pltpu.roll(bcum_col, LANES - HEADS, axis=1)

        yield
        state = state_ref[...]
        qs = _dot(qb, state.astype(bf16))

        row_i = lax.broadcasted_iota(jnp.int32, (t, t), 0)
        col_i = lax.broadcasted_iota(jnp.int32, (t, t), 1)
        causal = col_i <= row_i
        m_prev = mlane_ref[...]
        num = jnp.zeros((t, g), f32)
        den_col = jnp.zeros((t, LANES), f32)
        mcol = jnp.zeros((t, LANES), f32)
        wgt_rows = []
        for h in range(HEADS):
            a_h = a_row[h:h + 1, :]
            masked = jnp.where(causal, a_h, -jnp.inf)
            m_h = jnp.maximum(jnp.max(masked, axis=1, keepdims=True), m_prev[:, h:h + 1])
            p_h = scores[h] * jnp.exp(masked - m_h)
            den_col = jnp.where(lane_c == h, jnp.sum(p_h, axis=1, keepdims=True), den_col)
            mcol = jnp.where(lane_c == h, m_h, mcol)
            num = num + _dot(p_h.astype(bf16), jnp.where(head_g == h, vb, jnp.zeros_like(vb)))
            wgt_rows.append(jnp.exp(a_h - m_h[t - 1:t, :]))

        wexp = jnp.concatenate([jnp.broadcast_to(w, (dh, t)) for w in wgt_rows], axis=0)
        kwb = (kt * wexp).astype(bf16)
        v_aug = jnp.concatenate([vb, jnp.ones((t, LANES), bf16)], axis=1)
        upd = _dot(kwb, v_aug)
        m_last = mcol[t - 1:t, :]
        decay_l = jnp.exp(m_prev - m_last)
        decay = jnp.concatenate([_expand_heads(decay_l, dh, g), decay_l], axis=1)
        r_head = lax.broadcasted_iota(jnp.int32, (g, g + LANES), 0) // dh
        c_idx = lax.broadcasted_iota(jnp.int32, (g, g + LANES), 1)
        c_head = jnp.where(c_idx < g, c_idx // dh, c_idx - g)
        state_ref[...] = state * decay + jnp.where(r_head == c_head, upd, 0.0)
        mlane_ref[...] = bcol[t - 1:t, :] + m_last
        y_b = _dot(pooled.astype(bf16), poolw_ref[...]) * pscale_ref[...]
        y_c = _dot(sw.astype(bf16), pww_ref[...]) + pwb_ref[...]

        yield

        inter = jnp.exp(m_prev - mcol)
        den = den_col + inter * qs[:, g:]
        dmax = jnp.maximum(jnp.abs(den), jnp.exp(-(bcol + mcol)))
        num = num + _expand_heads(inter, dh, g) * qs[:, :g]
        sq = num * num
        ss = jnp.zeros((t, LANES), f32)
        head_tg = lax.broadcasted_iota(jnp.int32, (t, g), 1) // dh
        for h in range(HEADS):
            ss = jnp.where(lane_c == h, jnp.sum(jnp.where(head_tg == h, sq, 0.0), axis=1, keepdims=True), ss)
        inv_d = 1.0 / dmax
        scale = inv_d * lax.rsqrt((ss * (1.0 / dh)) * inv_d * inv_d + EPS)
        y_a = _sigmoid(o_pre) * (num * _expand_heads(scale, dh, g) * mgain_ref[...])

        mixed = jnp.concatenate([y_a.astype(bf16), y_b.astype(bf16), y_c.astype(bf16), y_d.astype(bf16)], axis=1)
        yield
        hand_ref[rows, :] = x + _dot(mixed, wout_ref[...])

    x1 = hand_ref[...]
    h1b = (x1 * lax.rsqrt(jnp.mean(x1 * x1, axis=-1, keepdims=True) + EPS) * mlpg_ref[...]).astype(bf16)
    tiles = [run_tile(sub) for sub in range(SUBTILES)]
    for tile in tiles:
        next(tile)
    act = jnp.maximum(_dot(h1b, wup_ref[...]), 0.0)
    actb = (act * act).astype(bf16)
    for tile in tiles:
        next(tile)
    y = x1 + _dot(actb, wdown_ref[...])
    if final_norm:
        y = y * lax.rsqrt(jnp.mean(y * y, axis=-1, keepdims=True) + EPS) * fg_ref[...]
    o_ref[...] = y
    for tile in tiles:
        next(tile)
    for tile in tiles:
        next(tile, None)


def _shared_spec(a):
    return pl.BlockSpec(a.shape, lambda s: (0,) * a.ndim)


def _layer_spec(a, layer):
    a, width, k = a if isinstance(a, tuple) else (a, a.shape[-1], 0)
    block = (None,) + a.shape[1:-1] + (width,)
    index_map = lambda s: (layer,) + (0,) * (a.ndim - 2) + (k,)
    if a.size // a.shape[0] // a.shape[-1] * width * a.dtype.itemsize >= BIG_OPERAND_BYTES:
        return pl.BlockSpec(block, index_map, pipeline_mode=pl.Buffered(1))
    return pl.BlockSpec(block, index_map)


def _layer_call(x2, seq_len, layer, shared, stacked, final_norm):
    n, d = x2.shape
    g = stacked[5].shape[-1]
    t = shared[0].shape[0]
    rows = SUBTILES * t
    assert (seq_len // t) % SUBTILES == 0 and n % rows == 0
    blocks = n // rows
    return pl.pallas_call(
        functools.partial(_layer_kernel, tiles_per_seq=seq_len // t, final_norm=final_norm),
        grid=(blocks + 1,),
        in_specs=[pl.BlockSpec((rows, d), lambda s: (jnp.minimum(s, blocks - 1), 0))]
        + [_shared_spec(a) for a in shared] + [_layer_spec(a, layer) for a in stacked],
        out_specs=pl.BlockSpec((rows, d), lambda s: (jnp.maximum(s - 1, 0), 0)),
        out_shape=jax.ShapeDtypeStruct(x2.shape, x2.dtype),
        scratch_shapes=[
            pltpu.VMEM((rows, d), f32),
            pltpu.VMEM((g, g + LANES), f32),
            pltpu.VMEM((1, LANES), f32),
            pltpu.VMEM((CONV_HALO + t, g), f32),
            pltpu.VMEM((SUBLANES - 1, CONV_HALO + t, g), f32),
            pltpu.VMEM((POOL_HALO + t, g), f32),
            pltpu.VMEM((SCONV_HALO + t, g), f32),
        ],
        compiler_params=pltpu.CompilerParams(
            dimension_semantics=("arbitrary",), vmem_limit_bytes=VMEM_LIMIT_BYTES),
        name="layer_final" if final_norm else "layer",
    )(x2, *shared, *[a[0] if isinstance(a, tuple) else a for a in stacked])


def kernel(x, attn_norm_gain, w_in, mlstm_igate_bias, mlstm_fgate_bias, mlstm_norm_gain, pool_w, pool_scale, conv_w, conv_b, conv_ln_gain, conv_ln_bias, conv_pw_w, conv_pw_b, sconv_w, w_out, mlp_norm_gain, w_up, w_down, final_norm_gain):
    depth = w_in.shape[0]
    b, s, d = x.shape
    g = mlstm_norm_gain.shape[1]
    t = min(SEQ_TILE, s)
    assert s % t == 0 and t % LANES == 0
    gate_lo = 4 * g
    gate_hi = gate_lo + 2 * HEADS
    gate_pad = LANES - 2 * HEADS
    assert gate_lo % LANES == 0 and gate_lo + LANES <= w_in.shape[2]
    w_in_b = w_in.astype(bf16)
    row = lambda a: a[:, None, :]
    pool_groups = pool_w.shape[1]
    pool_bd = (jnp.eye(pool_groups, dtype=pool_w.dtype)[None, :, None, :, None]
               * pool_w[:, :, :, None, :]).reshape(depth, g, g)
    stacked = (
        row(attn_norm_gain),
        (w_in_b, gate_lo, 0),
        w_in_b[:, :, gate_hi:],
        (w_in_b, LANES, gate_lo // LANES),
        row(jnp.pad(jnp.concatenate([mlstm_igate_bias, mlstm_fgate_bias], axis=1), ((0, 0), (0, gate_pad)))),
        row(mlstm_norm_gain), pool_bd.astype(bf16), row(pool_scale),
        conv_w, row(conv_b), row(conv_ln_gain), row(conv_ln_bias),
        conv_pw_w.astype(bf16), row(conv_pw_b), sconv_w, w_out.astype(bf16),
        row(mlp_norm_gain), w_up.astype(bf16), w_down.astype(bf16))
    shared = (jnp.triu(jnp.ones((t, t), bf16)),
              final_norm_gain[None, :])
    x2 = x.reshape(b * s, d)
    for l in range(depth):
        x2 = _layer_call(x2, s, l, shared, stacked, final_norm=(l == depth - 1))
    return x2.reshape(b, s, d)
```

```python
import functools

import jax
import jax.numpy as jnp
from jax import lax
from jax.experimental import pallas as pl
from jax.experimental.pallas import tpu as pltpu

EPS = 1e-6
HEADS = 4
POOL_WINDOWS = (2, 4, 8, 16)
assert all(w & (w - 1) == 0 for w in POOL_WINDOWS)
CONV_K = 31
SCONV_K = 3
LANES = 128
SUBLANES = 8
SEQ_TILE = 256
SUBTILES = 2
BIG_OPERAND_BYTES = 1 << 20
CONV_HALO = 32
POOL_HALO = 16
SCONV_HALO = 8
CONV_ROWS = 32
VMEM_LIMIT_BYTES = 56 * 1024 * 1024

f32 = jnp.float32
bf16 = jnp.bfloat16


def _dot(a, b):
    return jnp.dot(a, b, preferred_element_type=f32)


def _sigmoid(x):
    return 1.0 / (1.0 + jnp.exp(-x))


def _log_sigmoid(x):
    return jnp.minimum(x, 0.0) - jnp.log(1.0 + jnp.exp(-jnp.abs(x)))


def _split3(x):
    hi = x.astype(bf16).astype(f32)
    rest = x - hi
    mid = rest.astype(bf16).astype(f32)
    return hi, mid, rest - mid


def _expand_heads(col, group, width):
    t = col.shape[0]
    head = lax.broadcasted_iota(jnp.int32, (t, width), 1) // group
    out = jnp.broadcast_to(col[:, HEADS - 1:HEADS], (t, width))
    for h in range(HEADS - 2, -1, -1):
        out = jnp.where(head == h, jnp.broadcast_to(col[:, h:h + 1], (t, width)), out)
    return out


def _layer_kernel(x_ref, trit_ref, fg_ref, ng_ref, wqkvo_ref, wrest_ref, wgate_ref, gbias_ref, mgain_ref,
                  poolw_ref, pscale_ref, convw_ref, convb_ref, lng_ref, lnb_ref, pww_ref, pwb_ref, scw_ref,
                  wout_ref, mlpg_ref, wup_ref, wdown_ref,
                  o_ref,
                  hand_ref, state_ref, mlane_ref, ubuf_ref, cbuf_ref, pbuf_ref, sbuf_ref,
                  *, tiles_per_seq, final_norm):
    t = x_ref.shape[0] // SUBTILES
    g = mgain_ref.shape[1]
    dh = g // HEADS
    step = pl.program_id(0)
    j = lax.rem(step * SUBTILES, tiles_per_seq)

    @pl.when(step == 0)
    def _():
        hand_ref[...] = jnp.zeros_like(hand_ref)

    @pl.when(j == 0)
    def _():
        state_ref[...] = jnp.zeros_like(state_ref)
        mlane_ref[...] = jnp.zeros_like(mlane_ref)
        ubuf_ref[0:CONV_HALO, :] = jnp.zeros((CONV_HALO, g), f32)
        pbuf_ref[0:POOL_HALO, :] = jnp.zeros((POOL_HALO, g), f32)
        sbuf_ref[0:SCONV_HALO, :] = jnp.zeros((SCONV_HALO, g), f32)

    def run_tile(sub):
        rows = slice(sub * t, (sub + 1) * t)
        j_sub = j + sub
        x = x_ref[rows, :]
        hn = x * lax.rsqrt(jnp.mean(x * x, axis=-1, keepdims=True) + EPS) * ng_ref[...]
        hb = hn.astype(bf16)

        def proj(i):
            if i < 4:
                return _dot(hb, wqkvo_ref[:, i * g:(i + 1) * g])
            return _dot(hb, wrest_ref[:, (i - 4) * g:(i - 3) * g])

        off = CONV_HALO - (CONV_K - 1)
        tap_src = []
        for tap in range(CONV_K):
            b = (off + tap) % SUBLANES
            tap_src.append((ubuf_ref, off + tap) if b == 0 else (cbuf_ref.at[b - 1], off + tap - b))

        def conv_copies():
            for b in range(1, SUBLANES):
                n_rows = max(r for tap, (_, r) in enumerate(tap_src) if (off + tap) % SUBLANES == b) + t
                cbuf_ref[b - 1, 0:n_rows, :] = ubuf_ref[b:b + n_rows, :]

        def conv_block(r0):
            acc = jnp.broadcast_to(convb_ref[...], (CONV_ROWS, g))
            for tap, (src, r) in enumerate(tap_src):
                acc = acc + convw_ref[tap:tap + 1, :] * src[r + r0:r + r0 + CONV_ROWS, :]
            return acc

        gates = _dot(hb, wgate_ref[...]) + gbias_ref[...]

        x1 = hand_ref[rows, :]
        h1b = (x1 * lax.rsqrt(jnp.mean(x1 * x1, axis=-1, keepdims=True) + EPS) * mlpg_ref[...]).astype(bf16)

        glu = proj(5) * _sigmoid(proj(6))
        ubuf_ref[CONV_HALO:CONV_HALO + t, :] = glu
        lane_c = lax.broadcasted_iota(jnp.int32, (t, LANES), 1)
        gcol = jnp.where(lane_c < HEADS, gates, _log_sigmoid(gates))
        grow = gcol.T[0:SUBLANES, :]
        rows3 = jnp.concatenate(list(_split3(grow)) + [jnp.zeros((SUBLANES, t), f32)], axis=0).astype(bf16)
        pool_in = proj(4)
        pbuf_ref[POOL_HALO:POOL_HALO + t, :] = pool_in
        cx = proj(8) * proj(9)
        sbuf_ref[SCONV_HALO:SCONV_HALO + t, :] = cx
        sc_b = proj(7)
        k = proj(1) * (dh ** -0.5)
        q = proj(0)

        conv_copies()
        cacc = jnp.concatenate([conv_block(r0) for r0 in range(0, t, CONV_ROWS)], axis=0)
        ubuf_ref[0:CONV_HALO, :] = ubuf_ref[t:t + CONV_HALO, :]
        mu = jnp.mean(cacc, axis=-1, keepdims=True)
        cen = cacc - mu
        var = jnp.mean(cen * cen, axis=-1, keepdims=True)
        ln = cen * lax.rsqrt(var + EPS) * lng_ref[...] + lnb_ref[...]
        sw = ln * _sigmoid(ln)
        v = proj(2)

        ext = pbuf_ref[...]
        win = []
        acc = ext
        shift = 1
        while shift < max(POOL_WINDOWS):
            acc = acc + pltpu.roll(acc, shift, axis=0)
            shift *= 2
            win.append(acc[POOL_HALO:POOL_HALO + t, :])
        gc = g // len(POOL_WINDOWS)
        grp = lax.broadcasted_iota(jnp.int32, (t, g), 1) // gc
        wsum = win[-1]
        wlen = jnp.full((1, g), float(POOL_WINDOWS[-1]), f32)
        for gi in range(len(POOL_WINDOWS) - 2, -1, -1):
            wsum = jnp.where(grp == gi, win[gi], wsum)
            wlen = jnp.where(grp[0:1, :] == gi, float(POOL_WINDOWS[gi]), wlen)
        pooled = wsum * (1.0 / wlen) - pool_in
        edge = max(POOL_WINDOWS)
        pos = (lax.broadcasted_iota(jnp.int32, (edge, g), 0) + 1).astype(f32)
        first = wsum[0:edge, :] / jnp.minimum(pos, wlen) - pool_in[0:edge, :]
        pooled = jnp.concatenate([jnp.where(j_sub == 0, first, pooled[0:edge, :]), pooled[edge:, :]], axis=0)
        pbuf_ref[0:POOL_HALO, :] = pbuf_ref[t:t + POOL_HALO, :]
        o_pre = proj(3)

        sacc = jnp.zeros((t, g), f32)
        soff = SCONV_HALO - (SCONV_K - 1)
        for tap in range(SCONV_K):
            sacc = sacc + scw_ref[tap:tap + 1, :] * sbuf_ref[soff + tap:soff + tap + t, :]
        sbuf_ref[0:SCONV_HALO, :] = sbuf_ref[t:t + SCONV_HALO, :]
        y_d = sc_b * sacc
        yield

        qb = q.astype(bf16)
        vb = v.astype(bf16)
        kt = k.T
        ktb = kt.astype(bf16)
        head_g = lax.broadcasted_iota(jnp.int32, (1, g), 1) // dh
        scores = []
        for h in range(HEADS):
            parts = [jnp.zeros((h * dh, t), bf16), ktb[h * dh:(h + 1) * dh, :], jnp.zeros((g - (h + 1) * dh, t), bf16)]
            scores.append(_dot(qb, jnp.concatenate([p for p in parts if p.shape[0]], axis=0)))
        state = state_ref[...]
        qs = _dot(qb, state.astype(bf16))
        rsum = _dot(rows3, trit_ref[...])
        bcum_row = rsum[0:SUBLANES] + rsum[SUBLANES:2 * SUBLANES] + rsum[2 * SUBLANES:3 * SUBLANES]
        a_row = grow - pltpu.roll(bcum_row, HEADS, axis=0)
        bcum_col = jnp.concatenate([bcum_row, jnp.zeros((LANES - SUBLANES, t), f32)], axis=0).T
        bcol = pltpu.roll(bcum_col, LANES - HEADS, axis=1)

        act = jnp.maximum(_dot(h1b, wup_ref[...]), 0.0)
        actb = (act * act).astype(bf16)

        row_i = lax.broadcasted_iota(jnp.int32, (t, t), 0)
        col_i = lax.broadcasted_iota(jnp.int32, (t, t), 1)
        causal = col_i <= row_i
        m_prev = mlane_ref[...]
        num = jnp.zeros((t, g), f32)
        den_col = jnp.zeros((t, LANES), f32)
        mcol = jnp.zeros((t, LANES), f32)
        wgt_rows = []
        for h in range(HEADS):
            a_h = a_row[h:h + 1, :]
            masked = jnp.where(causal, a_h, -jnp.inf)
            m_h = jnp.maximum(jnp.max(masked, axis=1, keepdims=True), m_prev[:, h:h + 1])
            p_h = scores[h] * jnp.exp(masked - m_h)
            den_col = jnp.where(lane_c == h, jnp.sum(p_h, axis=1, keepdims=True), den_col)
            mcol = jnp.where(lane_c == h, m_h, mcol)
            num = num + _dot(p_h.astype(bf16), jnp.where(head_g == h, vb, jnp.zeros_like(vb)))
            wgt_rows.append(jnp.exp(a_h - m_h[t - 1:t, :]))

        wexp = jnp.concatenate([jnp.broadcast_to(w, (dh, t)) for w in wgt_rows], axis=0)
        kwb = (kt * wexp).astype(bf16)
        v_aug = jnp.concatenate([vb, jnp.ones((t, LANES), bf16)], axis=1)
        upd = _dot(kwb, v_aug)
        m_last = mcol[t - 1:t, :]
        decay_l = jnp.exp(m_prev - m_last)
        decay = jnp.concatenate([_expand_heads(decay_l, dh, g), decay_l], axis=1)
        r_head = lax.broadcasted_iota(jnp.int32, (g, g + LANES), 0) // dh
        c_idx = lax.broadcasted_iota(jnp.int32, (g, g + LANES), 1)
        c_head = jnp.where(c_idx < g, c_idx // dh, c_idx - g)
        state_ref[...] = state * decay + jnp.where(r_head == c_head, upd, 0.0)
        mlane_ref[...] = bcol[t - 1:t, :] + m_last
        y_b = _dot(pooled.astype(bf16), poolw_ref[...]) * pscale_ref[...]
        y_c = _dot(sw.astype(bf16), pww_ref[...]) + pwb_ref[...]

        y = x1 + _dot(actb, wdown_ref[...])
        if final_norm:
            y = y * lax.rsqrt(jnp.mean(y * y, axis=-1, keepdims=True) + EPS) * fg_ref[...]
        o_ref[rows, :] = y

        inter = jnp.exp(m_prev - mcol)
        den = den_col + inter * qs[:, g:]
        dmax = jnp.maximum(jnp.abs(den), jnp.exp(-(bcol + mcol)))
        num = num + _expand_heads(inter, dh, g) * qs[:, :g]
        sq = num * num
        ss = jnp.zeros((t, LANES), f32)
        head_tg = lax.broadcasted_iota(jnp.int32, (t, g), 1) // dh
        for h in range(HEADS):
            ss = jnp.where(lane_c == h, jnp.sum(jnp.where(head_tg == h, sq, 0.0), axis=1, keepdims=True), ss)
        inv_d = 1.0 / dmax
        scale = inv_d * lax.rsqrt((ss * (1.0 / dh)) * inv_d * inv_d + EPS)
        y_a = _sigmoid(o_pre) * (num * _expand_heads(scale, dh, g) * mgain_ref[...])

        mixed = jnp.concatenate([y_a.astype(bf16), y_b.astype(bf16), y_c.astype(bf16), y_d.astype(bf16)], axis=1)
        yield
        hand_ref[rows, :] = x + _dot(mixed, wout_ref[...])

    tiles = [run_tile(sub) for sub in range(SUBTILES)]
    for phase in range(3):
        for tile in tiles:
            next(tile, None)


def _shared_spec(a):
    return pl.BlockSpec(a.shape, lambda s: (0,) * a.ndim)


def _layer_spec(a, layer):
    a, width, k = a if isinstance(a, tuple) else (a, a.shape[-1], 0)
    block = (None,) + a.shape[1:-1] + (width,)
    index_map = lambda s: (layer,) + (0,) * (a.ndim - 2) + (k,)
    if a.size // a.shape[0] // a.shape[-1] * width * a.dtype.itemsize >= BIG_OPERAND_BYTES:
        return pl.BlockSpec(block, index_map, pipeline_mode=pl.Buffered(1))
    return pl.BlockSpec(block, index_map)


def _layer_call(x2, seq_len, layer, shared, stacked, final_norm):
    n, d = x2.shape
    g = stacked[5].shape[-1]
    t = shared[0].shape[0]
    rows = SUBTILES * t
    assert (seq_len // t) % SUBTILES == 0 and n % rows == 0
    blocks = n // rows
    return pl.pallas_call(
        functools.partial(_layer_kernel, tiles_per_seq=seq_len // t, final_norm=final_norm),
        grid=(blocks + 1,),
        in_specs=[pl.BlockSpec((rows, d), lambda s: (jnp.minimum(s, blocks - 1), 0))]
        + [_shared_spec(a) for a in shared] + [_layer_spec(a, layer) for a in stacked],
        out_specs=pl.BlockSpec((rows, d), lambda s: (jnp.maximum(s - 1, 0), 0)),
        out_shape=jax.ShapeDtypeStruct(x2.shape, x2.dtype),
        scratch_shapes=[
            pltpu.VMEM((rows, d), f32),
            pltpu.VMEM((g, g + LANES), f32),
            pltpu.VMEM((1, LANES), f32),
            pltpu.VMEM((CONV_HALO + t, g), f32),
            pltpu.VMEM((SUBLANES - 1, CONV_HALO + t, g), f32),
            pltpu.VMEM((POOL_HALO + t, g), f32),
            pltpu.VMEM((SCONV_HALO + t, g), f32),
        ],
        compiler_params=pltpu.CompilerParams(
            dimension_semantics=("arbitrary",), vmem_limit_bytes=VMEM_LIMIT_BYTES),
        name="layer_final" if final_norm else "layer",
    )(x2, *shared, *[a[0] if isinstance(a, tuple) else a for a in stacked])


def kernel(x, attn_norm_gain, w_in, mlstm_igate_bias, mlstm_fgate_bias, mlstm_norm_gain, pool_w, pool_scale, conv_w, conv_b, conv_ln_gain, conv_ln_bias, conv_pw_w, conv_pw_b, sconv_w, w_out, mlp_norm_gain, w_up, w_down, final_norm_gain):
    depth = w_in.shape[0]
    b, s, d = x.shape
    g = mlstm_norm_gain.shape[1]
    t = min(SEQ_TILE, s)
    assert s % t == 0 and t % LANES == 0
    gate_lo = 4 * g
    gate_hi = gate_lo + 2 * HEADS
    gate_pad = LANES - 2 * HEADS
    assert gate_lo % LANES == 0 and gate_lo + LANES <= w_in.shape[2]
    w_in_b = w_in.astype(bf16)
    row = lambda a: a[:, None, :]
    pool_groups = pool_w.shape[1]
    pool_bd = (jnp.eye(pool_groups, dtype=pool_w.dtype)[None, :, None, :, None]
               * pool_w[:, :, :, None, :]).reshape(depth, g, g)
    stacked = (
        row(attn_norm_gain),
        (w_in_b, gate_lo, 0),
        w_in_b[:, :, gate_hi:],
        (w_in_b, LANES, gate_lo // LANES),
        row(jnp.pad(jnp.concatenate([mlstm_igate_bias, mlstm_fgate_bias], axis=1), ((0, 0), (0, gate_pad)))),
        row(mlstm_norm_gain), pool_bd.astype(bf16), row(pool_scale),
        conv_w, row(conv_b), row(conv_ln_gain), row(conv_ln_bias),
        conv_pw_w.astype(bf16), row(conv_pw_b), sconv_w, w_out.astype(bf16),
        row(mlp_norm_gain), w_up.astype(bf16), w_down.astype(bf16))
    shared = (jnp.triu(jnp.ones((t, t), bf16)),
              final_norm_gain[None, :])
    x2 = x.reshape(b * s, d)
    for l in range(depth):
        x2 = _layer_call(x2, s, l, shared, stacked, final_norm=(l == depth - 1))
    return x2.reshape(b, s, d)
```
